```python
import math
import jax, jax.numpy as jnp
from jax import lax
import numpy as np


D_MODEL = 1024
BATCH = 8
SEQ = 2048
DEPTH = 4
DEC_BATCH = 16
DEC_SEQ = 64
PAST_LEN = 1024

CHUNK = 64
N_HEADS = 24
N_KV_HEADS = 3
HEAD_DIM = 64
GQA_GROUP = N_HEADS // N_KV_HEADS
WINDOW = 128
WINDOW_CHUNKS = WINDOW // CHUNK
Q_WIDTH = N_HEADS * HEAD_DIM
KV_WIDTH = N_KV_HEADS * HEAD_DIM
N_MEM = 256
MEM_HEADS = 4
MEM_HEAD_DIM = 128
MEM_WIDTH = MEM_HEADS * MEM_HEAD_DIM
SSM_INNER = 1536
SSM_HEAD_DIM = 64
SSM_HEADS = SSM_INNER // SSM_HEAD_DIM
SSM_GROUPS = 8
D_STATE = 128
CONV_W = 4
CONV_DIM = SSM_INNER + 2 * SSM_GROUPS * D_STATE
MIX_WIDTH = Q_WIDTH + MEM_WIDTH
D_FF = 4 * D_MODEL
IN_ATTN = Q_WIDTH + 2 * KV_WIDTH + MEM_WIDTH
IN_SSM = SSM_INNER + CONV_DIM + SSM_HEADS + MEM_WIDTH
N_ATTN_LAYERS = (DEPTH + 1) // 2
N_SSM_LAYERS = DEPTH // 2
EPS = 1e-6

kernel_name = 'hybrid_swa_ssd_memory_stream_step'


def rmsnorm(x, w):
    xf = x.astype(jnp.float32)
    y = xf * lax.rsqrt(jnp.mean(xf * xf, axis=-1, keepdims=True) + EPS)
    return (y * w.astype(jnp.float32)).astype(x.dtype)


def alibi_slopes():
    s = 2.0 ** (-8.0 * np.arange(1, N_HEADS + 1) / N_HEADS)
    return jnp.asarray(s, dtype=jnp.float32).reshape(N_KV_HEADS, GQA_GROUP)


def sink_probs(scores, sink):
    m = jnp.maximum(jnp.max(scores, axis=-1, keepdims=True), sink)
    p = jnp.exp(scores - m)
    return p / (jnp.sum(p, axis=-1, keepdims=True) + jnp.exp(sink - m))


def swa_prompt(q, k, v, sink):
    b, l = q.shape[:2]
    nc = l // CHUNK
    span = (WINDOW_CHUNKS + 1) * CHUNK
    qb = q.reshape(b, nc, CHUNK, N_KV_HEADS, GQA_GROUP, HEAD_DIM)
    pad = ((0, 0), (WINDOW_CHUNKS * CHUNK, 0), (0, 0), (0, 0))
    kp = jnp.pad(k, pad).reshape(b, nc + WINDOW_CHUNKS, CHUNK, N_KV_HEADS, HEAD_DIM)
    vp = jnp.pad(v, pad).reshape(b, nc + WINDOW_CHUNKS, CHUNK, N_KV_HEADS, HEAD_DIM)
    kb = jnp.concatenate([kp[:, w:w + nc] for w in range(WINDOW_CHUNKS + 1)], axis=2)
    vb = jnp.concatenate([vp[:, w:w + nc] for w in range(WINDOW_CHUNKS + 1)], axis=2)
    s = jnp.einsum('bcqkgd,bcskd->bckgqs', qb, kb).astype(jnp.float32) / math.sqrt(HEAD_DIM)
    dist = WINDOW_CHUNKS * CHUNK + jnp.arange(CHUNK)[:, None] - jnp.arange(span)[None, :]
    bias = -alibi_slopes()[:, :, None, None] * jnp.abs(dist).astype(jnp.float32)
    key_pos = (jnp.arange(nc)[:, None] - WINDOW_CHUNKS) * CHUNK + jnp.arange(span)[None, :]
    valid = (key_pos >= 0)[None, :, None, None, None, :]
    s = jnp.where(valid, s + bias[None, None], -jnp.inf)
    sk = sink.astype(jnp.float32).reshape(N_KV_HEADS, GQA_GROUP)[None, None, :, :, None, None]
    p = sink_probs(s, sk)
    o = jnp.einsum('bckgqs,bcskd->bcqkgd', p.astype(v.dtype), vb)
    return o.reshape(b, l, Q_WIDTH)


def swa_sample(q, k, v, cache_k, cache_v, sink):
    b, l = q.shape[:2]
    keep = cache_k.shape[1]
    kk = jnp.concatenate([cache_k.astype(k.dtype), k], axis=1)
    vv = jnp.concatenate([cache_v.astype(v.dtype), v], axis=1)
    qh = q.reshape(b, l, N_KV_HEADS, GQA_GROUP, HEAD_DIM)
    s = jnp.einsum('bqkgd,bskd->bkgqs', qh, kk).astype(jnp.float32) / math.sqrt(HEAD_DIM)
    dist = keep + jnp.arange(l)[:, None] - jnp.arange(keep + l)[None, :]
    bias = -alibi_slopes()[:, :, None, None] * jnp.abs(dist).astype(jnp.float32)
    sk = sink.astype(jnp.float32).reshape(N_KV_HEADS, GQA_GROUP)[None, :, :, None, None]
    p = sink_probs(s + bias[None], sk)
    o = jnp.einsum('bkgqs,bskd->bqkgd', p.astype(v.dtype), vv)
    return o.reshape(b, l, Q_WIDTH)


def cross_attend(cq, mem_k, mem_v):
    b, l = cq.shape[:2]
    qh = cq.reshape(b, l, MEM_HEADS, MEM_HEAD_DIM)
    s = jnp.einsum('bqhd,bmhd->bhqm', qh, mem_k.astype(cq.dtype)).astype(jnp.float32) / math.sqrt(MEM_HEAD_DIM)
    p = jax.nn.softmax(s, axis=-1)
    o = jnp.einsum('bhqm,bmhd->bqhd', p.astype(cq.dtype), mem_v.astype(cq.dtype))
    return o.reshape(b, l, MEM_WIDTH)


def ssd(x, dt, A, Bm, Cm, h0, block):
    b, l = x.shape[:2]
    hg = SSM_HEADS // SSM_GROUPS
    nc = l // block
    xf = x.astype(jnp.float32).reshape(b, nc, block, SSM_GROUPS, hg, SSM_HEAD_DIM)
    dtf = dt.astype(jnp.float32).reshape(b, nc, block, SSM_GROUPS, hg)
    Bf = Bm.astype(jnp.float32).reshape(b, nc, block, SSM_GROUPS, D_STATE)
    Cf = Cm.astype(jnp.float32).reshape(b, nc, block, SSM_GROUPS, D_STATE)
    acs = jnp.cumsum(dtf * A.astype(jnp.float32).reshape(SSM_GROUPS, hg), axis=2)
    xdt = xf * dtf[..., None]
    causal = jnp.tril(jnp.ones((block, block), dtype=bool))[:, :, None, None]
    diff = acs[:, :, :, None] - acs[:, :, None, :]
    decay = jnp.exp(jnp.where(causal, diff, -jnp.inf))
    cb = jnp.einsum('bcqgn,bcsgn->bcqsg', Cf, Bf)
    y_diag = jnp.einsum('bcqsg,bcqsgh,bcsghp->bcqghp', cb, decay, xdt)
    decay_end = jnp.exp(acs[:, :, -1:] - acs)
    states = jnp.einsum('bcsgn,bcsgh,bcsghp->bcghpn', Bf, decay_end, xdt)
    block_decay = jnp.exp(acs[:, :, -1])

    def step(h, inp):
        s_c, d_c = inp
        return h * d_c[..., None, None] + s_c, h

    h_init = h0.astype(jnp.float32).reshape(b, SSM_GROUPS, hg, SSM_HEAD_DIM, D_STATE)
    h_last, h_in = lax.scan(step, h_init, (jnp.moveaxis(states, 1, 0), jnp.moveaxis(block_decay, 1, 0)))
    h_in = jnp.moveaxis(h_in, 0, 1)
    y_off = jnp.einsum('bcqgn,bcghpn,bcqgh->bcqghp', Cf, h_in, jnp.exp(acs))
    y = (y_diag + y_off).reshape(b, l, SSM_HEADS, SSM_HEAD_DIM)
    return y.astype(x.dtype), h_last.reshape(b, SSM_HEADS, SSM_HEAD_DIM, D_STATE)


def ssm_mixer(z, xbc, dt_raw, conv_hist, h0, conv_w, conv_b, dt_bias, a_log, d_skip, norm_w, block):
    b, l = z.shape[:2]
    xpad = jnp.concatenate([conv_hist.astype(xbc.dtype), xbc], axis=1)
    new_conv = xpad[:, xpad.shape[1] - (CONV_W - 1):]
    conv = lax.conv_general_dilated(xpad, conv_w[:, None, :].astype(xbc.dtype), window_strides=(1,), padding='VALID',
                                    dimension_numbers=('NWC', 'WIO', 'NWC'), feature_group_count=CONV_DIM)
    u = jax.nn.silu(conv + conv_b.astype(xbc.dtype))
    xs, Bm, Cm = jnp.split(u, [SSM_INNER, SSM_INNER + SSM_GROUPS * D_STATE], axis=-1)
    dt = jax.nn.softplus(dt_raw.astype(jnp.float32) + dt_bias.astype(jnp.float32))
    A = -jnp.exp(a_log.astype(jnp.float32))
    xh = xs.reshape(b, l, SSM_HEADS, SSM_HEAD_DIM)
    y, h_last = ssd(xh, dt, A, Bm.reshape(b, l, SSM_GROUPS, D_STATE), Cm.reshape(b, l, SSM_GROUPS, D_STATE), h0, block)
    y = (y + d_skip.astype(xs.dtype)[:, None] * xh).reshape(b, l, SSM_INNER) * jax.nn.silu(z)
    y = rmsnorm(y.reshape(b, l, SSM_GROUPS, SSM_INNER // SSM_GROUPS), norm_w.reshape(SSM_GROUPS, SSM_INNER // SSM_GROUPS))
    return y.reshape(b, l, SSM_INNER), new_conv, h_last


def trunk(x, mem_k, mem_v, swa_k_cache, swa_v_cache, ssm_cache, conv_cache, prompt, p):
    b, l, _ = x.shape
    new_k, new_v, new_h, new_conv = [], [], [], []
    for i in range(DEPTH):
        j = i // 2
        h = rmsnorm(x, p['norm_mix_pre'][i])
        if i % 2 == 0:
            proj = h @ p['w_in_attn'][j]
            q, k, v, cq = jnp.split(proj, [Q_WIDTH, Q_WIDTH + KV_WIDTH, Q_WIDTH + 2 * KV_WIDTH], axis=-1)
            q = q.reshape(b, l, N_HEADS, HEAD_DIM)
            k = k.reshape(b, l, N_KV_HEADS, HEAD_DIM)
            v = v.reshape(b, l, N_KV_HEADS, HEAD_DIM)
            if prompt:
                a = swa_prompt(q, k, v, p['attn_sinks'][j])
                keep = min(WINDOW, l)
                new_k.append(k[:, l - keep:])
                new_v.append(v[:, l - keep:])
            else:
                a = swa_sample(q, k, v, swa_k_cache[j], swa_v_cache[j], p['attn_sinks'][j])
                new_k.append(k)
                new_v.append(v)
        else:
            proj = h @ p['w_in_ssm'][j]
            z, xbc, dt_raw, cq = jnp.split(proj, [SSM_INNER, SSM_INNER + CONV_DIM, SSM_INNER + CONV_DIM + SSM_HEADS], axis=-1)
            if prompt:
                hist = jnp.zeros((b, CONV_W - 1, CONV_DIM), x.dtype)
                h0 = jnp.zeros((b, SSM_HEADS, SSM_HEAD_DIM, D_STATE), jnp.float32)
                block = CHUNK
            else:
                hist = conv_cache[j]
                h0 = ssm_cache[j]
                block = l
            a, cs, hs = ssm_mixer(z, xbc, dt_raw, hist, h0, p['conv_w'][j], p['conv_b'][j], p['dt_bias'][j],
                                  p['a_log'][j], p['d_skip'][j], p['ssm_norm'][j], block)
            new_conv.append(cs)
            new_h.append(hs)
        c = cross_attend(cq, mem_k[i], mem_v[i])
        mix = jnp.concatenate([a, c], axis=-1) @ p['w_out'][i]
        x = x + rmsnorm(mix, p['norm_mix_post'][i])
        h = rmsnorm(x, p['norm_ffn_pre'][i])
        f = jnp.square(jax.nn.relu(h @ p['w_up'][i])) @ p['w_down'][i]
        x = x + rmsnorm(f, p['norm_ffn_post'][i])
    return x, jnp.stack(new_k), jnp.stack(new_v), jnp.stack(new_h), jnp.stack(new_conv)


def setup_inputs(seed: int = 0) -> dict:
    key = jax.random.key(seed)
    ks = jax.random.split(key, 32)
    f32 = jnp.float32

    def nrm(k, shape, scale):
        return jax.random.normal(k, shape, f32) * scale

    swa_keep = min(WINDOW, PAST_LEN)
    dt0 = jnp.exp(jax.random.uniform(ks[19], (N_SSM_LAYERS, SSM_HEADS), f32, math.log(1e-3), math.log(1e-1)))
    return {
        'x_prompt': nrm(ks[0], (BATCH, SEQ, D_MODEL), 1.0),
        'x_sample': nrm(ks[1], (DEC_BATCH, DEC_SEQ, D_MODEL), 1.0),
        'mem_prompt': nrm(ks[2], (BATCH, N_MEM, D_MODEL), 1.0),
        'cache_swa_k': nrm(ks[3], (N_ATTN_LAYERS, DEC_BATCH, swa_keep, N_KV_HEADS, HEAD_DIM), 1.0),
        'cache_swa_v': nrm(ks[4], (N_ATTN_LAYERS, DEC_BATCH, swa_keep, N_KV_HEADS, HEAD_DIM), 1.0),
        'state_ssm': nrm(ks[5], (N_SSM_LAYERS, DEC_BATCH, SSM_HEADS, SSM_HEAD_DIM, D_STATE), 0.1),
        'state_conv': nrm(ks[6], (N_SSM_LAYERS, DEC_BATCH, CONV_W - 1, CONV_DIM), 1.0),
        'cache_mem_k': nrm(ks[7], (DEPTH, DEC_BATCH, N_MEM, MEM_HEADS, MEM_HEAD_DIM), 1.0),
        'cache_mem_v': nrm(ks[8], (DEPTH, DEC_BATCH, N_MEM, MEM_HEADS, MEM_HEAD_DIM), 1.0),
        'norm_mix_pre': 1.0 + nrm(ks[9], (DEPTH, D_MODEL), 0.05),
        'norm_mix_post': 1.0 + nrm(ks[10], (DEPTH, D_MODEL), 0.05),
        'norm_ffn_pre': 1.0 + nrm(ks[11], (DEPTH, D_MODEL), 0.05),
        'norm_ffn_post': 1.0 + nrm(ks[12], (DEPTH, D_MODEL), 0.05),
        'w_in_attn': nrm(ks[13], (N_ATTN_LAYERS, D_MODEL, IN_ATTN), D_MODEL ** -0.5),
        'attn_sinks': nrm(ks[14], (N_ATTN_LAYERS, N_HEADS), 0.5),
        'w_in_ssm': nrm(ks[15], (N_SSM_LAYERS, D_MODEL, IN_SSM), D_MODEL ** -0.5),
        'conv_w': nrm(ks[16], (N_SSM_LAYERS, CONV_W, CONV_DIM), 0.5),
        'conv_b': nrm(ks[17], (N_SSM_LAYERS, CONV_DIM), 0.02),
        'dt_bias': dt0 + jnp.log(-jnp.expm1(-dt0)),
        'a_log': jnp.log(jax.random.uniform(ks[18], (N_SSM_LAYERS, SSM_HEADS), f32, 1.0, 16.0)),
        'd_skip': 1.0 + nrm(ks[20], (N_SSM_LAYERS, SSM_HEADS), 0.1),
        'ssm_norm': 1.0 + nrm(ks[21], (N_SSM_LAYERS, SSM_INNER), 0.05),
        'mem_norm': 1.0 + nrm(ks[22], (DEPTH, D_MODEL), 0.05),
        'w_mem_kv': nrm(ks[23], (DEPTH, D_MODEL, 2 * MEM_WIDTH), D_MODEL ** -0.5),
        'w_out': nrm(ks[24], (DEPTH, MIX_WIDTH, D_MODEL), MIX_WIDTH ** -0.5),
        'w_up': nrm(ks[25], (DEPTH, D_MODEL, D_FF), D_MODEL ** -0.5),
        'w_down': nrm(ks[26], (DEPTH, D_FF, D_MODEL), D_FF ** -0.5),
    }


def reference(x_prompt, x_sample, mem_prompt, cache_swa_k, cache_swa_v, state_ssm, state_conv, cache_mem_k, cache_mem_v,
              norm_mix_pre, norm_mix_post, norm_ffn_pre, norm_ffn_post, w_in_attn, attn_sinks, w_in_ssm, conv_w, conv_b,
              dt_bias, a_log, d_skip, ssm_norm, mem_norm, w_mem_kv, w_out, w_up, w_down):
    p = dict(norm_mix_pre=norm_mix_pre, norm_mix_post=norm_mix_post, norm_ffn_pre=norm_ffn_pre,
             norm_ffn_post=norm_ffn_post, w_in_attn=w_in_attn, attn_sinks=attn_sinks, w_in_ssm=w_in_ssm,
             conv_w=conv_w, conv_b=conv_b, dt_bias=dt_bias, a_log=a_log, d_skip=d_skip, ssm_norm=ssm_norm,
             w_out=w_out, w_up=w_up, w_down=w_down)
    bp = mem_prompt.shape[0]
    mk, mv = [], []
    for i in range(DEPTH):
        kv = rmsnorm(mem_prompt, mem_norm[i]) @ w_mem_kv[i]
        k_i, v_i = jnp.split(kv, [MEM_WIDTH], axis=-1)
        mk.append(k_i.reshape(bp, N_MEM, MEM_HEADS, MEM_HEAD_DIM))
        mv.append(v_i.reshape(bp, N_MEM, MEM_HEADS, MEM_HEAD_DIM))
    prompt_mem_k = jnp.stack(mk)
    prompt_mem_v = jnp.stack(mv)
    y_prompt, prompt_swa_k, prompt_swa_v, prompt_ssm, prompt_conv = trunk(
        x_prompt, prompt_mem_k, prompt_mem_v, None, None, None, None, True, p)
    y_sample, sample_swa_k, sample_swa_v, sample_ssm, sample_conv = trunk(
        x_sample, cache_mem_k, cache_mem_v, cache_swa_k, cache_swa_v, state_ssm, state_conv, False, p)
    return (y_prompt, y_sample, prompt_swa_k, prompt_swa_v, prompt_ssm, prompt_conv, prompt_mem_k, prompt_mem_v,
            sample_swa_k, sample_swa_v, sample_ssm, sample_conv)
```

```python
import functools
import math

import numpy as np
import jax
import jax.numpy as jnp
from jax import lax
from jax.experimental import pallas as pl
from jax.experimental.pallas import tpu as pltpu

F32 = jnp.float32
BF16 = jnp.bfloat16

D_MODEL = 1024
DEPTH = 4
CHUNK = 64
N_HEADS = 24
N_KV_HEADS = 3
HEAD_DIM = 64
WINDOW = 128
Q_WIDTH = N_HEADS * HEAD_DIM
KV_WIDTH = N_KV_HEADS * HEAD_DIM
N_MEM = 256
MEM_HEADS = 4
MEM_HEAD_DIM = 128
MEM_WIDTH = MEM_HEADS * MEM_HEAD_DIM
SSM_INNER = 1536
SSM_HEAD_DIM = 64
SSM_HEADS = SSM_INNER // SSM_HEAD_DIM
SSM_GROUPS = 8
SSM_GROUP_WIDTH = SSM_INNER // SSM_GROUPS
D_STATE = 128
CONV_W = 4
CONV_DIM = SSM_INNER + 2 * SSM_GROUPS * D_STATE
D_FF = 4 * D_MODEL
EPS = 1e-6

LANES = 128
HEAD_PAIRS = SSM_HEADS // 2
NEG = -1e30
VMEM_LIMIT = 56 * 1024 * 1024

NT_DIMS = (((1,), (1,)), ((), ()))


def _rms(x, w):
    return x * lax.rsqrt(jnp.mean(x * x, axis=-1, keepdims=True) + EPS) * w


def _sigmoid(x):
    return 1.0 / (1.0 + jnp.exp(-x))


def _split3(x):
    a = x.astype(BF16)
    r = x - a.astype(F32)
    b = r.astype(BF16)
    c = (r - b.astype(F32)).astype(BF16)
    return a, b, c


def _resident(shape):
    return pl.BlockSpec(shape, lambda *_: (0,) * len(shape), pipeline_mode=pl.Buffered(1))


def _norm_matmul_kernel(x_ref, nw_ref, w_ref, *o_refs, segs):
    h = _rms(x_ref[...], nw_ref[...]).astype(BF16)
    for o_ref, (start, width) in zip(o_refs, segs):
        for c0 in range(0, width, 512):
            cw = min(512, width - c0)
            o_ref[:, c0:c0 + cw] = jnp.dot(
                h, w_ref[:, start + c0:start + c0 + cw], preferred_element_type=F32).astype(o_ref.dtype)


def _norm_matmul(x, nw, w, segs, dtypes, tm):
    m, k = x.shape
    n = w.shape[1]
    return pl.pallas_call(
        functools.partial(_norm_matmul_kernel, segs=segs),
        grid=(m // tm,),
        in_specs=[pl.BlockSpec((tm, k), lambda i: (i, 0)), _resident((1, k)), _resident((k, n))],
        out_specs=[pl.BlockSpec((tm, wd), lambda i: (i, 0)) for _, wd in segs],
        out_shape=[jax.ShapeDtypeStruct((m, wd), dt) for (_, wd), dt in zip(segs, dtypes)],
        compiler_params=pltpu.CompilerParams(dimension_semantics=("parallel",), vmem_limit_bytes=VMEM_LIMIT),
        name="norm_matmul",
    )(x, nw.reshape(1, k), w)


def _attn_bias_table(sinks):
    slopes = jnp.asarray(2.0 ** (-8.0 * np.arange(1, N_HEADS + 1) / N_HEADS), dtype=F32).reshape(N_KV_HEADS, 4, 1, 2, 1)
    span = WINDOW + CHUNK
    i = np.arange(CHUNK)[:, None]
    s = np.arange(2 * LANES)[None, :]
    dist = jnp.asarray(np.abs(WINDOW + i - s), dtype=F32)[None, None, :, None, :]
    slot = jnp.asarray(s)[None, None, :, None, :]
    bias = -slopes * dist
    bias = jnp.where(slot < span, bias, NEG)
    bias = jnp.where(slot == span, sinks.astype(F32).reshape(N_KV_HEADS, 4, 1, 2, 1), bias)
    return bias.reshape(N_KV_HEADS, 4 * CHUNK, 4 * LANES)


def _attn_kernel(q_ref, kvc_ref, kvp_ref, bias_ref, o_ref, kvbuf, *, nb, T, mask_first):
    first = pl.program_id(1) == 0
    lo = lax.broadcasted_iota(jnp.int32, (1, LANES), 1) < HEAD_DIM
    slot = lax.broadcasted_iota(jnp.int32, (1, 4 * LANES), 1) & (2 * LANES - 1)
    zpad = jnp.zeros((CHUNK, LANES), BF16)
    span = WINDOW + CHUNK
    for n in range(nb):
        kvbuf[0:WINDOW, :] = kvp_ref[n]
        kvbuf[WINDOW:WINDOW + T, :] = kvc_ref[n]
        for j in range(T // CHUNK):
            win = kvbuf[j * CHUNK:j * CHUNK + span, :]
            blocks = [win[:, m * LANES:(m + 1) * LANES] for m in range(3)]
            rolled = [pltpu.roll(b, HEAD_DIM, 1) for b in blocks]

            def halves(idx):
                m, half = divmod(idx, 2)
                in_lo, in_hi = (blocks[m], rolled[m]) if half == 0 else (rolled[m], blocks[m])
                return jnp.where(lo, in_lo, 0.0).astype(BF16), jnp.where(lo, 0.0, in_hi).astype(BF16)

            for kvh in range(N_KV_HEADS):
                klo, khi = halves(kvh)
                vlo, vhi = halves(N_KV_HEADS + kvh)
                kbd = jnp.concatenate([klo, zpad, khi, zpad], axis=0)
                c0 = kvh * 4 * LANES
                qs = jnp.concatenate(
                    [q_ref[n, j * CHUNK:(j + 1) * CHUNK, c0 + p * LANES:c0 + (p + 1) * LANES] for p in range(4)],
                    axis=0)
                s = lax.dot_general(qs, kbd, NT_DIMS, preferred_element_type=F32) * (1.0 / math.sqrt(HEAD_DIM))
                s = s + bias_ref[kvh]
                if mask_first and j < WINDOW // CHUNK:
                    s = jnp.where(slot < jnp.where(first, WINDOW - j * CHUNK, 0), NEG, s)
                o = None
                for e, vh in enumerate((vlo, vhi)):
                    se = s[:, e * 2 * LANES:(e + 1) * 2 * LANES]
                    p = jnp.exp(se - jnp.max(se, axis=-1, keepdims=True))
                    den = jnp.sum(p, axis=-1, keepdims=True)
                    oe = jnp.dot(p.astype(BF16), jnp.concatenate([vh, zpad], axis=0), preferred_element_type=F32)
                    oe = oe * (1.0 / den)
                    o = oe if o is None else o + oe
                for p in range(4):
                    o_ref[n, j * CHUNK:(j + 1) * CHUNK, c0 + p * LANES:c0 + (p + 1) * LANES] = (
                        o[p * CHUNK:(p + 1) * CHUNK].astype(o_ref.dtype))


def _attention(q, kv, kv_prev, bias, nb, T):
    b, l, _ = q.shape
    mask_first = kv_prev is None
    if mask_first:
        prev_arr = kv
        prev_spec = pl.BlockSpec((nb, WINDOW, 2 * KV_WIDTH),
                                 lambda bi, i: (bi, jnp.maximum(i * (T // WINDOW) - 1, 0), 0))
    else:
        prev_arr = kv_prev
        prev_spec = pl.BlockSpec((nb, WINDOW, 2 * KV_WIDTH), lambda bi, i: (bi, 0, 0))
    return pl.pallas_call(
        functools.partial(_attn_kernel, nb=nb, T=T, mask_first=mask_first),
        grid=(b // nb, l // T),
        in_specs=[pl.BlockSpec((nb, T, Q_WIDTH), lambda bi, i: (bi, i, 0)),
                  pl.BlockSpec((nb, T, 2 * KV_WIDTH), lambda bi, i: (bi, i, 0)),
                  prev_spec,
                  _resident(bias.shape)],
        out_specs=pl.BlockSpec((nb, T, Q_WIDTH), lambda bi, i: (bi, i, 0)),
        out_shape=jax.ShapeDtypeStruct((b, l, Q_WIDTH), BF16),
        scratch_shapes=[pltpu.VMEM((WINDOW + T, 2 * KV_WIDTH), F32)],
        compiler_params=pltpu.CompilerParams(dimension_semantics=("parallel", "parallel"),
                                             vmem_limit_bytes=VMEM_LIMIT),
        name="swa_attention",
    )(q, kv, prev_arr, bias)


def _ssd_kernel(z_ref, xbc_ref, dt_ref, hist_ref, h0_ref, cw_ref, cb_ref, dtb_ref, alog_ref, dsk_ref, nw_ref,
                e64_ref, y_ref, hout_ref, cbuf, ubuf, ybuf, dtx_s, acs_s, st, *, T):
    i = pl.program_id(1)

    @pl.when(i == 0)
    def _init():
        cbuf[0:8, :] = hist_ref[0]
        for hp in range(HEAD_PAIRS):
            st[hp] = h0_ref[0, 2 * hp:2 * hp + 2].reshape(2 * SSM_HEAD_DIM, D_STATE).T

    cbuf[8:8 + T, :] = xbc_ref[0]

    row = lax.broadcasted_iota(jnp.int32, (CHUNK, LANES), 0)
    pos = lax.broadcasted_iota(jnp.int32, (CHUNK, LANES), 1) & (CHUNK - 1)
    diag = pos == row
    causal = pos <= row
    lo = lax.broadcasted_iota(jnp.int32, (1, LANES), 1) < SSM_HEAD_DIM
    tri = (lax.broadcasted_iota(jnp.int32, (CHUNK, CHUNK), 1)
           <= lax.broadcasted_iota(jnp.int32, (CHUNK, CHUNK), 0)).astype(F32).astype(BF16)
    a_neg = -jnp.exp(alog_ref[...])
    b_off = SSM_INNER
    c_off = SSM_INNER + SSM_GROUPS * D_STATE

    for j in range(T // CHUNK):
        r0 = 8 + j * CHUNK
        rows = slice(j * CHUNK, (j + 1) * CHUNK)
        for c0 in range(0, CONV_DIM, 512):
            cols = slice(c0, c0 + 512)
            acc = cbuf[r0 - 3:r0 - 3 + CHUNK, cols] * cw_ref[0:1, cols]
            for w in range(1, CONV_W):
                acc = acc + cbuf[r0 - 3 + w:r0 - 3 + w + CHUNK, cols] * cw_ref[w:w + 1, cols]
            acc = acc + cb_ref[:, cols]
            ubuf[:, cols] = acc * _sigmoid(acc)

        dt_in = dt_ref[0, rows, :] + dtb_ref[...]
        dtv = jnp.maximum(dt_in, 0.0) + jnp.log1p(jnp.exp(-jnp.abs(dt_in)))
        cum3 = jnp.dot(tri, jnp.concatenate(_split3(dtv * a_neg), axis=1), preferred_element_type=F32)
        cum = cum3[:, 2 * LANES:] + cum3[:, LANES:2 * LANES] + cum3[:, :LANES]
        big = jnp.dot(jnp.concatenate(_split3(dtv) + _split3(cum), axis=0), e64_ref[...],
                      preferred_element_type=F32)
        dtx_s[...] = big[128:192] + big[64:128] + big[0:64]
        acs_s[...] = big[320:384] + big[256:320] + big[192:256]

        bb, bt, cb, cbd = [], [], [], []
        for g in range(SSM_GROUPS):
            bg = ubuf[:, b_off + g * D_STATE:b_off + (g + 1) * D_STATE]
            bb.append(bg.astype(BF16))
            bt.append(bg.T.astype(BF16))
            cb.append(ubuf[:, c_off + g * D_STATE:c_off + (g + 1) * D_STATE].astype(BF16))
            cbd.append(lax.dot_general(cb[g], jnp.concatenate([bb[g], bb[g]], axis=0), NT_DIMS,
                                       preferred_element_type=F32))

        for hp in range(HEAD_PAIRS):
            sl = slice(hp * LANES, (hp + 1) * LANES)
            g0, g1 = (2 * hp) // 3, (2 * hp + 1) // 3
            acs = acs_s[:, sl]
            dtx = dtx_s[:, sl]
            acs_key = jnp.sum(jnp.where(diag, acs, 0.0), axis=0, keepdims=True)
            dt_key = jnp.sum(jnp.where(diag, dtx, 0.0), axis=0, keepdims=True)
            lmat = jnp.exp(jnp.where(causal, acs - acs_key, NEG)) * dt_key
            eacs = jnp.exp(acs)
            acs_last = acs[CHUNK - 1:CHUNK, :]
            xp = ubuf[:, sl]
            xlo = jnp.where(lo, xp, 0.0)
            xhi = jnp.where(lo, 0.0, xp)

            cbp = cbd[g0] if g0 == g1 else jnp.where(lo, cbd[g0], cbd[g1])
            xbd = jnp.concatenate([xlo, xhi], axis=0).astype(BF16)
            yd = jnp.dot((cbp * lmat).astype(BF16), xbd, preferred_element_type=F32)

            ps = st[hp]
            psb = ps.astype(BF16)
            yo = jnp.dot(cb[g0], psb, preferred_element_type=F32)
            if g0 != g1:
                yo = jnp.where(lo, yo, jnp.dot(cb[g1], psb, preferred_element_type=F32))
            ybuf[:, sl] = yd + yo * eacs + dsk_ref[:, sl] * xp

            wx = jnp.exp(acs_last - acs) * dtx
            if g0 == g1:
                new = jnp.dot(bt[g0], (xp * wx).astype(BF16), preferred_element_type=F32)
            else:
                new = (jnp.dot(bt[g0], (xlo * wx).astype(BF16), preferred_element_type=F32)
                       + jnp.dot(bt[g1], (xhi * wx).astype(BF16), preferred_element_type=F32))
            st[hp] = ps * eacs[CHUNK - 1:CHUNK, :] + new

        zc = z_ref[0, rows, :]
        yg = ybuf[...] * (zc * _sigmoid(zc))
        y2 = yg * yg
        rs = []
        for g in range(SSM_GROUPS):
            g_lo, g_hi = g * SSM_GROUP_WIDTH, (g + 1) * SSM_GROUP_WIDTH
            w0 = (g_lo // LANES) * LANES
            w1 = -(-g_hi // LANES) * LANES
            lane = lax.broadcasted_iota(jnp.int32, (1, w1 - w0), 1) + w0
            ss = jnp.sum(jnp.where(lane < g_lo, 0.0, jnp.where(lane < g_hi, y2[:, w0:w1], 0.0)),
                         axis=-1, keepdims=True)
            rs.append(lax.rsqrt(ss / SSM_GROUP_WIDTH + EPS))
        lane = lax.broadcasted_iota(jnp.int32, (1, LANES), 1)
        for v in range(SSM_INNER // LANES):
            ga, gb = (v * LANES) // SSM_GROUP_WIDTH, (v * LANES + LANES - 1) // SSM_GROUP_WIDTH
            sc = rs[ga] if ga == gb else jnp.where(lane < gb * SSM_GROUP_WIDTH - v * LANES, rs[ga], rs[gb])
            vs = slice(v * LANES, (v + 1) * LANES)
            y_ref[0, rows, vs] = (yg[:, vs] * sc * nw_ref[:, vs]).astype(y_ref.dtype)

    cbuf[0:8, :] = cbuf[T:T + 8, :]

    @pl.when(i == pl.num_programs(1) - 1)
    def _fin():
        for hp in range(HEAD_PAIRS):
            hout_ref[0, 2 * hp:2 * hp + 2] = st[hp].T.reshape(2, SSM_HEAD_DIM, D_STATE)


def _ssd(z, xbc, dt, hist, h0, p, T):
    b, l, _ = z.shape
    e64 = jnp.asarray(np.arange(LANES)[:, None] == (np.arange(SSM_INNER)[None, :] // SSM_HEAD_DIM), dtype=BF16)
    tok = lambda w: pl.BlockSpec((1, T, w), lambda bi, i: (bi, i, 0))
    return pl.pallas_call(
        functools.partial(_ssd_kernel, T=T),
        grid=(b, l // T),
        in_specs=[tok(SSM_INNER), tok(CONV_DIM), tok(LANES),
                  pl.BlockSpec((1, 8, CONV_DIM), lambda bi, i: (bi, 0, 0)),
                  pl.BlockSpec((1, SSM_HEADS, SSM_HEAD_DIM, D_STATE), lambda bi, i: (bi, 0, 0, 0)),
                  _resident((CONV_W, CONV_DIM)), _resident((1, CONV_DIM)), _resident((1, LANES)),
                  _resident((1, LANES)), _resident((1, SSM_INNER)), _resident((1, SSM_INNER)),
                  _resident((LANES, SSM_INNER))],
        out_specs=[tok(SSM_INNER),
                   pl.BlockSpec((1, SSM_HEADS, SSM_HEAD_DIM, D_STATE), lambda bi, i: (bi, 0, 0, 0))],
        out_shape=[jax.ShapeDtypeStruct((b, l, SSM_INNER), BF16),
                   jax.ShapeDtypeStruct((b, SSM_HEADS, SSM_HEAD_DIM, D_STATE), F32)],
        scratch_shapes=[pltpu.VMEM((8 + T, CONV_DIM), F32), pltpu.VMEM((CHUNK, CONV_DIM), F32),
                        pltpu.VMEM((CHUNK, SSM_INNER), F32), pltpu.VMEM((CHUNK, SSM_INNER), F32),
                        pltpu.VMEM((CHUNK, SSM_INNER), F32), pltpu.VMEM((HEAD_PAIRS, D_STATE, LANES), F32)],
        compiler_params=pltpu.CompilerParams(dimension_semantics=("parallel", "arbitrary"),
                                             vmem_limit_bytes=VMEM_LIMIT),
        name="conv_ssd",
    )(z, xbc, dt, hist, h0, p["conv_w"], p["conv_b"], p["dt_bias"], p["a_log"], p["d_skip"], p["ssm_norm"], e64)


def _tail_kernel(a_ref, cq_ref, mk_ref, mv_ref, x_ref, wout_ref, npost_ref, nfpre_ref, wup_ref, wdn_ref, nfpost_ref,
                 o_ref, *, nb, T):
    m = nb * T
    cs = []
    for n in range(nb):
        mk = mk_ref[n].astype(BF16)
        mv = mv_ref[n].astype(BF16)
        outs = []
        for h in range(MEM_HEADS):
            hs = slice(h * MEM_HEAD_DIM, (h + 1) * MEM_HEAD_DIM)
            s = lax.dot_general(cq_ref[n, :, hs], mk[:, hs], NT_DIMS, preferred_element_type=F32)
            s = s * (1.0 / math.sqrt(MEM_HEAD_DIM))
            p = jnp.exp(s - jnp.max(s, axis=-1, keepdims=True))
            den = jnp.sum(p, axis=-1, keepdims=True)
            o = jnp.dot(p.astype(BF16), mv[:, hs], preferred_element_type=F32) * (1.0 / den)
            outs.append(o.astype(BF16))
        cs.append(jnp.concatenate(outs, axis=1))
    c = cs[0] if nb == 1 else jnp.concatenate(cs, axis=0)
    a = a_ref[...].reshape(m, Q_WIDTH)
    mix = (jnp.dot(a, wout_ref[0:Q_WIDTH, :], preferred_element_type=F32)
           + jnp.dot(c, wout_ref[Q_WIDTH:, :], preferred_element_type=F32))
    x1 = x_ref[...].reshape(m, D_MODEL) + _rms(mix, npost_ref[...])
    h2 = _rms(x1, nfpre_ref[...]).astype(BF16)
    acc = None
    for f0 in range(0, D_FF, 1024):
        u = jnp.maximum(jnp.dot(h2, wup_ref[:, f0:f0 + 1024], preferred_element_type=F32), 0.0)
        d = jnp.dot((u * u).astype(BF16), wdn_ref[f0:f0 + 1024, :], preferred_element_type=F32)
        acc = d if acc is None else acc + d
    o_ref[...] = (x1 + _rms(acc, nfpost_ref[...])).reshape(nb, T, D_MODEL)


def _tail(a, cq, mk, mv, x, w_out, n_post, n_fpre, w_up, w_dn, n_fpost, nb, T):
    b, l, _ = x.shape
    tok = lambda w: pl.BlockSpec((nb, T, w), lambda bi, i: (bi, i, 0))
    mem = pl.BlockSpec((nb, N_MEM, MEM_WIDTH), lambda bi, i: (bi, 0, 0))
    vec = _resident((1, D_MODEL))
    return pl.pallas_call(
        functools.partial(_tail_kernel, nb=nb, T=T),
        grid=(b // nb, l // T),
        in_specs=[tok(Q_WIDTH), tok(MEM_WIDTH), mem, mem, tok(D_MODEL),
                  _resident(w_out.shape), vec, vec, _resident(w_up.shape), _resident(w_dn.shape), vec],
        out_specs=tok(D_MODEL),
        out_shape=jax.ShapeDtypeStruct((b, l, D_MODEL), F32),
        compiler_params=pltpu.CompilerParams(dimension_semantics=("parallel", "parallel"),
                                             vmem_limit_bytes=VMEM_LIMIT),
        name="xattn_outproj_mlp",
    )(a, cq, mk, mv, x, w_out, n_post.reshape(1, -1), n_fpre.reshape(1, -1), w_up, w_dn, n_fpost.reshape(1, -1))


ATTN_SEGS = ((0, Q_WIDTH), (Q_WIDTH, 2 * KV_WIDTH), (Q_WIDTH + 2 * KV_WIDTH, MEM_WIDTH))
SSM_SEGS = ((0, SSM_INNER), (SSM_INNER, CONV_DIM), (SSM_INNER + CONV_DIM, MEM_WIDTH),
            (SSM_INNER + CONV_DIM + MEM_WIDTH, LANES))


def _prep_weights(norm_mix_pre, norm_mix_post, norm_ffn_pre, norm_ffn_post, w_in_attn, attn_sinks, w_in_ssm, conv_w,
                  conv_b, dt_bias, a_log, d_skip, ssm_norm, w_out, w_up, w_down):
    dt_lo = SSM_INNER + CONV_DIM
    dt_hi = dt_lo + SSM_HEADS
    pad = jnp.zeros((w_in_ssm.shape[0], D_MODEL, LANES - SSM_HEADS), F32)
    w_ssm = jnp.concatenate([w_in_ssm[..., :dt_lo], w_in_ssm[..., dt_hi:], w_in_ssm[..., dt_lo:dt_hi], pad], axis=-1)
    lane_pad = lambda v: jnp.pad(v.astype(F32), ((0, 0), (0, LANES - SSM_HEADS))).reshape(-1, 1, LANES)
    return dict(
        norm_mix_pre=norm_mix_pre, norm_mix_post=norm_mix_post, norm_ffn_pre=norm_ffn_pre,
        norm_ffn_post=norm_ffn_post,
        w_in_attn=w_in_attn.astype(BF16), w_in_ssm=w_ssm.astype(BF16),
        attn_bias=[_attn_bias_table(attn_sinks[j]) for j in range(attn_sinks.shape[0])],
        conv_w=conv_w.astype(F32), conv_b=conv_b.astype(F32).reshape(-1, 1, CONV_DIM),
        dt_bias=lane_pad(dt_bias), a_log=lane_pad(a_log),
        d_skip=jnp.repeat(d_skip.astype(F32), SSM_HEAD_DIM, axis=-1).reshape(-1, 1, SSM_INNER),
        ssm_norm=ssm_norm.astype(F32).reshape(-1, 1, SSM_INNER),
        w_out=w_out.astype(BF16), w_up=w_up.astype(BF16), w_down=w_down.astype(BF16))


def _trunk(x, mem_k, mem_v, swa_prev, ssm_h0, conv_hist, p, *, tm, attn_tile, ssd_tile, tail_tile):
    b, l, _ = x.shape
    m = b * l
    new_kv, new_h, new_conv = [], [], []
    for i in range(DEPTH):
        j = i // 2
        xf = x.reshape(m, D_MODEL)
        if i % 2 == 0:
            q, kv, cq = _norm_matmul(xf, p["norm_mix_pre"][i], p["w_in_attn"][j], ATTN_SEGS, (BF16, F32, BF16), tm)
            kv = kv.reshape(b, l, 2 * KV_WIDTH)
            a = _attention(q.reshape(b, l, Q_WIDTH), kv, None if swa_prev is None else swa_prev[j],
                           p["attn_bias"][j], *attn_tile)
            new_kv.append(kv)
        else:
            z, xbc, cq, dt = _norm_matmul(xf, p["norm_mix_pre"][i], p["w_in_ssm"][j], SSM_SEGS,
                                          (F32, F32, BF16, F32), tm)
            xbc = xbc.reshape(b, l, CONV_DIM)
            pj = {k: p[k][j] for k in ("conv_w", "conv_b", "dt_bias", "a_log", "d_skip", "ssm_norm")}
            a, h_last = _ssd(z.reshape(b, l, SSM_INNER), xbc, dt.reshape(b, l, LANES), conv_hist[j], ssm_h0[j], pj,
                             ssd_tile)
            new_h.append(h_last)
            new_conv.append(xbc[:, l - (CONV_W - 1):])
        x = _tail(a, cq.reshape(b, l, MEM_WIDTH), mem_k[i], mem_v[i], x, p["w_out"][i], p["norm_mix_post"][i],
                  p["norm_ffn_pre"][i], p["w_up"][i], p["w_down"][i], p["norm_ffn_post"][i], *tail_tile)
    return x, new_kv, jnp.stack(new_h), jnp.stack(new_conv)


def kernel(x_prompt, x_sample, mem_prompt, cache_swa_k, cache_swa_v, state_ssm, state_conv, cache_mem_k, cache_mem_v,
           norm_mix_pre, norm_mix_post, norm_ffn_pre, norm_ffn_post, w_in_attn, attn_sinks, w_in_ssm, conv_w, conv_b,
           dt_bias, a_log, d_skip, ssm_norm, mem_norm, w_mem_kv, w_out, w_up, w_down):
    p = _prep_weights(norm_mix_pre, norm_mix_post, norm_ffn_pre, norm_ffn_post, w_in_attn, attn_sinks, w_in_ssm,
                      conv_w, conv_b, dt_bias, a_log, d_skip, ssm_norm, w_out, w_up, w_down)
    bp, lp, _ = x_prompt.shape
    bs, ls, _ = x_sample.shape
    n_attn = w_in_attn.shape[0]
    n_ssm = w_in_ssm.shape[0]

    mem_flat = mem_prompt.reshape(bp * N_MEM, D_MODEL)
    w_mem = w_mem_kv.astype(BF16)
    mk, mv = [], []
    for i in range(DEPTH):
        k_i, v_i = _norm_matmul(mem_flat, mem_norm[i], w_mem[i], ((0, MEM_WIDTH), (MEM_WIDTH, MEM_WIDTH)),
                                (F32, F32), 512)
        mk.append(k_i.reshape(bp, N_MEM, MEM_WIDTH))
        mv.append(v_i.reshape(bp, N_MEM, MEM_WIDTH))
    prompt_mem_k = jnp.stack(mk)
    prompt_mem_v = jnp.stack(mv)

    y_prompt, kv_p, prompt_ssm, prompt_conv = _trunk(
        x_prompt, prompt_mem_k, prompt_mem_v, None,
        jnp.zeros((n_ssm, bp, SSM_HEADS, SSM_HEAD_DIM, D_STATE), F32), jnp.zeros((n_ssm, bp, 8, CONV_DIM), F32), p,
        tm=512, attn_tile=(1, 256), ssd_tile=128, tail_tile=(1, 512))

    keep = cache_swa_k.shape[2]
    swa_prev = jnp.concatenate([cache_swa_k.reshape(n_attn, bs, keep, KV_WIDTH),
                                cache_swa_v.reshape(n_attn, bs, keep, KV_WIDTH)], axis=-1)
    hist = jnp.pad(state_conv, ((0, 0), (0, 0), (8 - (CONV_W - 1), 0), (0, 0)))
    y_sample, kv_s, sample_ssm, sample_conv = _trunk(
        x_sample, cache_mem_k.reshape(DEPTH, bs, N_MEM, MEM_WIDTH), cache_mem_v.reshape(DEPTH, bs, N_MEM, MEM_WIDTH),
        swa_prev, state_ssm, hist, p,
        tm=512, attn_tile=(4, ls), ssd_tile=ls, tail_tile=(8, ls))

    def split_kv(kvs, rows):
        kv = jnp.stack(kvs)[:, :, -rows:]
        shape = kv.shape[:3] + (N_KV_HEADS, HEAD_DIM)
        return kv[..., :KV_WIDTH].reshape(shape), kv[..., KV_WIDTH:].reshape(shape)

    prompt_swa_k, prompt_swa_v = split_kv(kv_p, min(WINDOW, lp))
    sample_swa_k, sample_swa_v = split_kv(kv_s, ls)
    mem_shape = (DEPTH, bp, N_MEM, MEM_HEADS, MEM_HEAD_DIM)
    return (y_prompt, y_sample, prompt_swa_k, prompt_swa_v, prompt_ssm, prompt_conv,
            prompt_mem_k.reshape(mem_shape), prompt_mem_v.reshape(mem_shape),
            sample_swa_k, sample_swa_v, sample_ssm, sample_conv)
```

```python
import functools
import math

import numpy as np
import jax
import jax.numpy as jnp
from jax import lax
from jax.experimental import pallas as pl
from jax.experimental.pallas import tpu as pltpu

F32 = jnp.float32
BF16 = jnp.bfloat16

D_MODEL = 1024
DEPTH = 4
CHUNK = 64
N_HEADS = 24
N_KV_HEADS = 3
HEAD_DIM = 64
WINDOW = 128
Q_WIDTH = N_HEADS * HEAD_DIM
KV_WIDTH = N_KV_HEADS * HEAD_DIM
N_MEM = 256
MEM_HEADS = 4
MEM_HEAD_DIM = 128
MEM_WIDTH = MEM_HEADS * MEM_HEAD_DIM
SSM_INNER = 1536
SSM_HEAD_DIM = 64
SSM_HEADS = SSM_INNER // SSM_HEAD_DIM
SSM_GROUPS = 8
SSM_GROUP_WIDTH = SSM_INNER // SSM_GROUPS
D_STATE = 128
CONV_W = 4
CONV_DIM = SSM_INNER + 2 * SSM_GROUPS * D_STATE
D_FF = 4 * D_MODEL
EPS = 1e-6

LANES = 128
HEAD_PAIRS = SSM_HEADS // 2
NEG = -1e30
VMEM_LIMIT = 56 * 1024 * 1024

NT_DIMS = (((1,), (1,)), ((), ()))


def _rms(x, w):
    return x * lax.rsqrt(jnp.mean(x * x, axis=-1, keepdims=True) + EPS) * w


def _sigmoid(x):
    return 1.0 / (1.0 + jnp.exp(-x))


def _split3(x):
    a = x.astype(BF16)
    r = x - a.astype(F32)
    b = r.astype(BF16)
    c = (r - b.astype(F32)).astype(BF16)
    return a, b, c


def _resident(shape):
    return pl.BlockSpec(shape, lambda *_: (0,) * len(shape), pipeline_mode=pl.Buffered(1))


def _layer(stacked, layer):
    shape = stacked.shape[1:]
    return pl.BlockSpec((None,) + shape, lambda *_: (layer,) + (0,) * len(shape), pipeline_mode=pl.Buffered(1))


def _norm_matmul_kernel(x_ref, nw_ref, *refs, plan):
    w_refs, o_refs = refs[:len(plan)], list(refs[len(plan):])
    h = _rms(x_ref[...], nw_ref[...]).astype(BF16)
    for w_ref, segs in zip(w_refs, plan):
        for start, width in segs:
            o_ref = o_refs.pop(0)
            for c0 in range(0, width, 512):
                cw = min(512, width - c0)
                o_ref[:, c0:c0 + cw] = jnp.dot(
                    h, w_ref[:, start + c0:start + c0 + cw], preferred_element_type=F32).astype(o_ref.dtype)


def _norm_matmul(x, nw, nw_layer, weights, w_layer, tm):
    m, k = x.shape
    plan = tuple(segs for _, segs, _ in weights)
    widths = [wd for _, segs, _ in weights for _, wd in segs]
    dtypes = [dt for _, _, dts in weights for dt in dts]
    return pl.pallas_call(
        functools.partial(_norm_matmul_kernel, plan=plan),
        grid=(m // tm,),
        in_specs=([pl.BlockSpec((tm, k), lambda i: (i, 0)), _layer(nw, nw_layer)]
                  + [_layer(w, w_layer) for w, _, _ in weights]),
        out_specs=[pl.BlockSpec((tm, wd), lambda i: (i, 0)) for wd in widths],
        out_shape=[jax.ShapeDtypeStruct((m, wd), dt) for wd, dt in zip(widths, dtypes)],
        compiler_params=pltpu.CompilerParams(dimension_semantics=("parallel",), vmem_limit_bytes=VMEM_LIMIT),
        name="norm_matmul",
    )(x, nw, *[w for w, _, _ in weights])


def _mem_kv_kernel(x_ref, nw_ref, w_ref, k_ref, v_ref):
    tm = x_ref.shape[0]
    h = _rms(x_ref[...], nw_ref[...]).astype(BF16)
    for o_ref, c0 in ((k_ref, 0), (v_ref, MEM_WIDTH)):
        r = jnp.dot(h, w_ref[:, c0:c0 + MEM_WIDTH], preferred_element_type=F32)
        for hd in range(MEM_HEADS):
            o_ref[pl.ds(hd, tm, stride=MEM_HEADS), :] = r[:, hd * MEM_HEAD_DIM:(hd + 1) * MEM_HEAD_DIM]


def _mem_kv(mem, nw, w, tm):
    m, k = mem.shape
    out = jax.ShapeDtypeStruct((DEPTH, m * MEM_HEADS, MEM_HEAD_DIM), F32)
    out_spec = pl.BlockSpec((None, tm * MEM_HEADS, MEM_HEAD_DIM), lambda l, i: (l, i, 0))
    return pl.pallas_call(
        _mem_kv_kernel,
        grid=(DEPTH, m // tm),
        in_specs=[pl.BlockSpec((tm, k), lambda l, i: (i, 0)),
                  pl.BlockSpec((None, 1, k), lambda l, i: (l, 0, 0)),
                  pl.BlockSpec((None, k, 2 * MEM_WIDTH), lambda l, i: (l, 0, 0))],
        out_specs=[out_spec, out_spec],
        out_shape=[out, out],
        compiler_params=pltpu.CompilerParams(dimension_semantics=("parallel", "parallel"),
                                             vmem_limit_bytes=VMEM_LIMIT),
        name="mem_kv",
    )(mem, nw, w)


def _attn_bias_table(sinks):
    slopes = jnp.asarray(2.0 ** (-8.0 * np.arange(1, N_HEADS + 1) / N_HEADS), dtype=F32).reshape(N_KV_HEADS, 4, 1, 2, 1)
    span = WINDOW + CHUNK
    i = np.arange(CHUNK)[:, None]
    s = np.arange(2 * LANES)[None, :]
    dist = jnp.asarray(np.abs(WINDOW + i - s), dtype=F32)[None, None, :, None, :]
    slot = jnp.asarray(s)[None, None, :, None, :]
    bias = -slopes * dist
    bias = jnp.where(slot < span, bias, NEG)
    bias = jnp.where(slot == span, sinks.astype(F32).reshape(N_KV_HEADS, 4, 1, 2, 1), bias)
    return bias.reshape(N_KV_HEADS, 4 * CHUNK, 4 * LANES)


def _attn_kernel(q_ref, kvc_ref, kvp_ref, bias_ref, o_ref, kvbuf, *, nb, T, mask_first):
    first = pl.program_id(1) == 0
    lo = lax.broadcasted_iota(jnp.int32, (1, LANES), 1) < HEAD_DIM
    slot = lax.broadcasted_iota(jnp.int32, (1, 4 * LANES), 1) & (2 * LANES - 1)
    zpad = jnp.zeros((CHUNK, LANES), BF16)
    span = WINDOW + CHUNK
    for n in range(nb):
        kvbuf[0:WINDOW, :] = kvp_ref[n]
        kvbuf[WINDOW:WINDOW + T, :] = kvc_ref[n]
        for j in range(T // CHUNK):
            win = kvbuf[j * CHUNK:j * CHUNK + span, :]
            blocks = [win[:, m * LANES:(m + 1) * LANES] for m in range(3)]
            rolled = [pltpu.roll(b, HEAD_DIM, 1) for b in blocks]

            def halves(idx):
                m, half = divmod(idx, 2)
                in_lo, in_hi = (blocks[m], rolled[m]) if half == 0 else (rolled[m], blocks[m])
                return jnp.where(lo, in_lo, 0.0).astype(BF16), jnp.where(lo, 0.0, in_hi).astype(BF16)

            for kvh in range(N_KV_HEADS):
                klo, khi = halves(kvh)
                vlo, vhi = halves(N_KV_HEADS + kvh)
                kbd = jnp.concatenate([klo, zpad, khi, zpad], axis=0)
                c0 = kvh * 4 * LANES
                qs = jnp.concatenate(
                    [q_ref[n, j * CHUNK:(j + 1) * CHUNK, c0 + p * LANES:c0 + (p + 1) * LANES] for p in range(4)],
                    axis=0)
                s = lax.dot_general(qs, kbd, NT_DIMS, preferred_element_type=F32) * (1.0 / math.sqrt(HEAD_DIM))
                s = s + bias_ref[kvh]
                if mask_first and j < WINDOW // CHUNK:
                    s = jnp.where(slot < jnp.where(first, WINDOW - j * CHUNK, 0), NEG, s)
                o = None
                for e, vh in enumerate((vlo, vhi)):
                    se = s[:, e * 2 * LANES:(e + 1) * 2 * LANES]
                    p = jnp.exp(se - jnp.max(se, axis=-1, keepdims=True))
                    den = jnp.sum(p, axis=-1, keepdims=True)
                    oe = jnp.dot(p.astype(BF16), jnp.concatenate([vh, zpad], axis=0), preferred_element_type=F32)
                    oe = oe * (1.0 / den)
                    o = oe if o is None else o + oe
                for p in range(4):
                    o_ref[n, j * CHUNK:(j + 1) * CHUNK, c0 + p * LANES:c0 + (p + 1) * LANES] = (
                        o[p * CHUNK:(p + 1) * CHUNK].astype(o_ref.dtype))


def _attention(q, kv, kv_prev, bias, layer, nb, T):
    b, l, _ = q.shape
    mask_first = kv_prev is None
    if mask_first:
        prev_arr = kv
        prev_spec = pl.BlockSpec((nb, WINDOW, 2 * KV_WIDTH),
                                 lambda bi, i: (bi, jnp.maximum(i * (T // WINDOW) - 1, 0), 0))
    else:
        prev_arr = kv_prev
        prev_spec = pl.BlockSpec((None, nb, WINDOW, 2 * KV_WIDTH), lambda bi, i: (layer, bi, 0, 0))
    return pl.pallas_call(
        functools.partial(_attn_kernel, nb=nb, T=T, mask_first=mask_first),
        grid=(b // nb, l // T),
        in_specs=[pl.BlockSpec((nb, T, Q_WIDTH), lambda bi, i: (bi, i, 0)),
                  pl.BlockSpec((nb, T, 2 * KV_WIDTH), lambda bi, i: (bi, i, 0)),
                  prev_spec,
                  _layer(bias, layer)],
        out_specs=pl.BlockSpec((nb, T, Q_WIDTH), lambda bi, i: (bi, i, 0)),
        out_shape=jax.ShapeDtypeStruct((b, l, Q_WIDTH), BF16),
        scratch_shapes=[pltpu.VMEM((WINDOW + T, 2 * KV_WIDTH), F32)],
        compiler_params=pltpu.CompilerParams(dimension_semantics=("parallel", "parallel"),
                                             vmem_limit_bytes=VMEM_LIMIT),
        name="swa_attention",
    )(q, kv, prev_arr, bias)


def _ssd_kernel(z_ref, xbc_ref, dt_ref, hist_ref, h0_ref, cw_ref, cb_ref, dtb_ref, alog_ref, dsk_ref, nw_ref,
                e64_ref, y_ref, hout_ref, cbuf, ubuf, ybuf, dtx_s, acs_s, st, *, T):
    i = pl.program_id(1)

    @pl.when(i == 0)
    def _init():
        cbuf[0:8, :] = hist_ref[0]
        for hp in range(HEAD_PAIRS):
            st[hp] = h0_ref[0, 2 * hp:2 * hp + 2].reshape(2 * SSM_HEAD_DIM, D_STATE).T

    cbuf[8:8 + T, :] = xbc_ref[0]

    row = lax.broadcasted_iota(jnp.int32, (CHUNK, LANES), 0)
    pos = lax.broadcasted_iota(jnp.int32, (CHUNK, LANES), 1) & (CHUNK - 1)
    diag = pos == row
    causal = pos <= row
    lo = lax.broadcasted_iota(jnp.int32, (1, LANES), 1) < SSM_HEAD_DIM
    tri = (lax.broadcasted_iota(jnp.int32, (CHUNK, CHUNK), 1)
           <= lax.broadcasted_iota(jnp.int32, (CHUNK, CHUNK), 0)).astype(F32).astype(BF16)
    a_neg = -jnp.exp(alog_ref[...])
    b_off = SSM_INNER
    c_off = SSM_INNER + SSM_GROUPS * D_STATE

    for j in range(T // CHUNK):
        r0 = 8 + j * CHUNK
        rows = slice(j * CHUNK, (j + 1) * CHUNK)
        for c0 in range(0, CONV_DIM, 512):
            cols = slice(c0, c0 + 512)
            acc = cbuf[r0 - 3:r0 - 3 + CHUNK, cols] * cw_ref[0:1, cols]
            for w in range(1, CONV_W):
                acc = acc + cbuf[r0 - 3 + w:r0 - 3 + w + CHUNK, cols] * cw_ref[w:w + 1, cols]
            acc = acc + cb_ref[:, cols]
            ubuf[:, cols] = acc * _sigmoid(acc)

        dt_in = dt_ref[0, rows, :] + dtb_ref[...]
        dtv = jnp.maximum(dt_in, 0.0) + jnp.log1p(jnp.exp(-jnp.abs(dt_in)))
        cum3 = jnp.dot(tri, jnp.concatenate(_split3(dtv * a_neg), axis=1), preferred_element_type=F32)
        cum = cum3[:, 2 * LANES:] + cum3[:, LANES:2 * LANES] + cum3[:, :LANES]
        big = jnp.dot(jnp.concatenate(_split3(dtv) + _split3(cum), axis=0), e64_ref[...],
                      preferred_element_type=F32)
        dtx_s[...] = big[128:192] + big[64:128] + big[0:64]
        acs_s[...] = big[320:384] + big[256:320] + big[192:256]

        bb, bt, cb, cbd = [], [], [], []
        for g in range(SSM_GROUPS):
            bg = ubuf[:, b_off + g * D_STATE:b_off + (g + 1) * D_STATE]
            bb.append(bg.astype(BF16))
            bt.append(bg.T.astype(BF16))
            cb.append(ubuf[:, c_off + g * D_STATE:c_off + (g + 1) * D_STATE].astype(BF16))
            cbd.append(lax.dot_general(cb[g], jnp.concatenate([bb[g], bb[g]], axis=0), NT_DIMS,
                                       preferred_element_type=F32))

        for hp in range(HEAD_PAIRS):
            sl = slice(hp * LANES, (hp + 1) * LANES)
            g0, g1 = (2 * hp) // 3, (2 * hp + 1) // 3
            acs = acs_s[:, sl]
            dtx = dtx_s[:, sl]
            acs_key = jnp.sum(jnp.where(diag, acs, 0.0), axis=0, keepdims=True)
            dt_key = jnp.sum(jnp.where(diag, dtx, 0.0), axis=0, keepdims=True)
            lmat = jnp.exp(jnp.where(causal, acs - acs_key, NEG)) * dt_key
            eacs = jnp.exp(acs)
            acs_last = acs[CHUNK - 1:CHUNK, :]
            xp = ubuf[:, sl]
            xlo = jnp.where(lo, xp, 0.0)
            xhi = jnp.where(lo, 0.0, xp)

            cbp = cbd[g0] if g0 == g1 else jnp.where(lo, cbd[g0], cbd[g1])
            xbd = jnp.concatenate([xlo, xhi], axis=0).astype(BF16)
            yd = jnp.dot((cbp * lmat).astype(BF16), xbd, preferred_element_type=F32)

            ps = st[hp]
            psb = ps.astype(BF16)
            yo = jnp.dot(cb[g0], psb, preferred_element_type=F32)
            if g0 != g1:
                yo = jnp.where(lo, yo, jnp.dot(cb[g1], psb, preferred_element_type=F32))
            ybuf[:, sl] = yd + yo * eacs + dsk_ref[:, sl] * xp

            wx = jnp.exp(acs_last - acs) * dtx
            if g0 == g1:
                new = jnp.dot(bt[g0], (xp * wx).astype(BF16), preferred_element_type=F32)
            else:
                new = (jnp.dot(bt[g0], (xlo * wx).astype(BF16), preferred_element_type=F32)
                       + jnp.dot(bt[g1], (xhi * wx).astype(BF16), preferred_element_type=F32))
            st[hp] = ps * eacs[CHUNK - 1:CHUNK, :] + new

        zc = z_ref[0, rows, :]
        yg = ybuf[...] * (zc * _sigmoid(zc))
        y2 = yg * yg
        rs = []
        for g in range(SSM_GROUPS):
            g_lo, g_hi = g * SSM_GROUP_WIDTH, (g + 1) * SSM_GROUP_WIDTH
            w0 = (g_lo // LANES) * LANES
            w1 = -(-g_hi // LANES) * LANES
            lane = lax.broadcasted_iota(jnp.int32, (1, w1 - w0), 1) + w0
            ss = jnp.sum(jnp.where(lane < g_lo, 0.0, jnp.where(lane < g_hi, y2[:, w0:w1], 0.0)),
                         axis=-1, keepdims=True)
            rs.append(lax.rsqrt(ss / SSM_GROUP_WIDTH + EPS))
        lane = lax.broadcasted_iota(jnp.int32, (1, LANES), 1)
        for v in range(SSM_INNER // LANES):
            ga, gb = (v * LANES) // SSM_GROUP_WIDTH, (v * LANES + LANES - 1) // SSM_GROUP_WIDTH
            sc = rs[ga] if ga == gb else jnp.where(lane < gb * SSM_GROUP_WIDTH - v * LANES, rs[ga], rs[gb])
            vs = slice(v * LANES, (v + 1) * LANES)
            y_ref[0, rows, vs] = (yg[:, vs] * sc * nw_ref[:, vs]).astype(y_ref.dtype)

    cbuf[0:8, :] = cbuf[T:T + 8, :]

    @pl.when(i == pl.num_programs(1) - 1)
    def _fin():
        for hp in range(HEAD_PAIRS):
            hout_ref[0, 2 * hp:2 * hp + 2] = st[hp].T.reshape(2, SSM_HEAD_DIM, D_STATE)


SSM_PARAMS = ("conv_w", "conv_b", "dt_bias", "a_log", "d_skip", "ssm_norm")


def _ssd(z, xbc, dt, hist, h0, p, layer, T):
    b, l, _ = z.shape
    e64 = jnp.asarray(np.arange(LANES)[:, None] == (np.arange(SSM_INNER)[None, :] // SSM_HEAD_DIM), dtype=BF16)
    tok = lambda w: pl.BlockSpec((1, T, w), lambda bi, i: (bi, i, 0))
    return pl.pallas_call(
        functools.partial(_ssd_kernel, T=T),
        grid=(b, l // T),
        in_specs=[tok(SSM_INNER), tok(CONV_DIM), tok(LANES),
                  pl.BlockSpec((None, 1, 8, CONV_DIM), lambda bi, i: (layer, bi, 0, 0)),
                  pl.BlockSpec((None, 1, SSM_HEADS, SSM_HEAD_DIM, D_STATE), lambda bi, i: (layer, bi, 0, 0, 0)),
                  *[_layer(p[k], layer) for k in SSM_PARAMS],
                  _resident((LANES, SSM_INNER))],
        out_specs=[tok(SSM_INNER),
                   pl.BlockSpec((1, SSM_HEADS, SSM_HEAD_DIM, D_STATE), lambda bi, i: (bi, 0, 0, 0))],
        out_shape=[jax.ShapeDtypeStruct((b, l, SSM_INNER), BF16),
                   jax.ShapeDtypeStruct((b, SSM_HEADS, SSM_HEAD_DIM, D_STATE), F32)],
        scratch_shapes=[pltpu.VMEM((8 + T, CONV_DIM), F32), pltpu.VMEM((CHUNK, CONV_DIM), F32),
                        pltpu.VMEM((CHUNK, SSM_INNER), F32), pltpu.VMEM((CHUNK, SSM_INNER), F32),
                        pltpu.VMEM((CHUNK, SSM_INNER), F32), pltpu.VMEM((HEAD_PAIRS, D_STATE, LANES), F32)],
        compiler_params=pltpu.CompilerParams(dimension_semantics=("parallel", "arbitrary"),
                                             vmem_limit_bytes=VMEM_LIMIT),
        name="conv_ssd",
    )(z, xbc, dt, hist, h0, *[p[k] for k in SSM_PARAMS], e64)


def _tail_kernel(a_ref, cq_ref, mk_ref, mv_ref, x_ref, wout_ref, npost_ref, nfpre_ref, wup_ref, wdn_ref, nfpost_ref,
                 o_ref, *, nb, T):
    m = nb * T
    cs = []
    for n in range(nb):
        outs = []
        for h in range(MEM_HEADS):
            hs = slice(h * MEM_HEAD_DIM, (h + 1) * MEM_HEAD_DIM)
            head_rows = pl.ds(h, N_MEM, stride=MEM_HEADS)
            mk = mk_ref[n, head_rows, :].astype(BF16)
            mv = mv_ref[n, head_rows, :].astype(BF16)
            s = lax.dot_general(cq_ref[n, :, hs], mk, NT_DIMS, preferred_element_type=F32)
            s = s * (1.0 / math.sqrt(MEM_HEAD_DIM))
            p = jnp.exp(s - jnp.max(s, axis=-1, keepdims=True))
            den = jnp.sum(p, axis=-1, keepdims=True)
            o = jnp.dot(p.astype(BF16), mv, preferred_element_type=F32) * (1.0 / den)
            outs.append(o.astype(BF16))
        cs.append(jnp.concatenate(outs, axis=1))
    c = cs[0] if nb == 1 else jnp.concatenate(cs, axis=0)
    a = a_ref[...].reshape(m, Q_WIDTH)
    mix = (jnp.dot(a, wout_ref[0:Q_WIDTH, :], preferred_element_type=F32)
           + jnp.dot(c, wout_ref[Q_WIDTH:, :], preferred_element_type=F32))
    x1 = x_ref[...].reshape(m, D_MODEL) + _rms(mix, npost_ref[...])
    h2 = _rms(x1, nfpre_ref[...]).astype(BF16)
    acc = None
    for f0 in range(0, D_FF, 1024):
        u = jnp.maximum(jnp.dot(h2, wup_ref[:, f0:f0 + 1024], preferred_element_type=F32), 0.0)
        d = jnp.dot((u * u).astype(BF16), wdn_ref[f0:f0 + 1024, :], preferred_element_type=F32)
        acc = d if acc is None else acc + d
    o_ref[...] = (x1 + _rms(acc, nfpost_ref[...])).reshape(nb, T, D_MODEL)


TAIL_PARAMS = ("w_out", "norm_mix_post", "norm_ffn_pre", "w_up", "w_down", "norm_ffn_post")


def _tail(a, cq, mk, mv, x, p, layer, nb, T):
    b, l, _ = x.shape
    tok = lambda w: pl.BlockSpec((nb, T, w), lambda bi, i: (bi, i, 0))
    mem = pl.BlockSpec((None, nb, N_MEM * MEM_HEADS, MEM_HEAD_DIM), lambda bi, i: (layer, bi, 0, 0))
    return pl.pallas_call(
        functools.partial(_tail_kernel, nb=nb, T=T),
        grid=(b // nb, l // T),
        in_specs=[tok(Q_WIDTH), tok(MEM_WIDTH), mem, mem, tok(D_MODEL),
                  *[_layer(p[k], layer) for k in TAIL_PARAMS]],
        out_specs=tok(D_MODEL),
        out_shape=jax.ShapeDtypeStruct((b, l, D_MODEL), F32),
        compiler_params=pltpu.CompilerParams(dimension_semantics=("parallel", "parallel"),
                                             vmem_limit_bytes=VMEM_LIMIT),
        name="xattn_outproj_mlp",
    )(a, cq, mk, mv, x, *[p[k] for k in TAIL_PARAMS])


ATTN_SEGS = ((0, Q_WIDTH), (Q_WIDTH, 2 * KV_WIDTH), (Q_WIDTH + 2 * KV_WIDTH, MEM_WIDTH))
SSM_MAIN_SEGS = ((0, SSM_INNER), (SSM_INNER, CONV_DIM))
SSM_SIDE_SEGS = ((0, MEM_WIDTH), (MEM_WIDTH, LANES))


def _prep_weights(norm_mix_pre, norm_mix_post, norm_ffn_pre, norm_ffn_post, w_in_attn, attn_sinks, w_in_ssm, conv_w,
                  conv_b, dt_bias, a_log, d_skip, ssm_norm, w_out, w_up, w_down):
    dt_lo = SSM_INNER + CONV_DIM
    dt_hi = dt_lo + SSM_HEADS
    pad = jnp.zeros((w_in_ssm.shape[0], D_MODEL, LANES - SSM_HEADS), F32)
    w_side = jnp.concatenate([w_in_ssm[..., dt_hi:], w_in_ssm[..., dt_lo:dt_hi], pad], axis=-1)
    vec = lambda v: v.astype(F32).reshape(v.shape[0], 1, -1)
    lane_pad = lambda v: vec(jnp.pad(v, ((0, 0), (0, LANES - SSM_HEADS))))
    return dict(
        norm_mix_pre=vec(norm_mix_pre), norm_mix_post=vec(norm_mix_post), norm_ffn_pre=vec(norm_ffn_pre),
        norm_ffn_post=vec(norm_ffn_post),
        w_in_attn=w_in_attn.astype(BF16), w_in_ssm=w_in_ssm[..., :dt_lo].astype(BF16), w_in_side=w_side.astype(BF16),
        attn_bias=jnp.stack([_attn_bias_table(attn_sinks[j]) for j in range(attn_sinks.shape[0])]),
        conv_w=conv_w.astype(F32), conv_b=vec(conv_b), dt_bias=lane_pad(dt_bias), a_log=lane_pad(a_log),
        d_skip=vec(jnp.repeat(d_skip, SSM_HEAD_DIM, axis=-1)), ssm_norm=vec(ssm_norm),
        w_out=w_out.astype(BF16), w_up=w_up.astype(BF16), w_down=w_down.astype(BF16))


def _trunk(x, mem_k, mem_v, swa_prev, ssm_h0, conv_hist, p, *, tm, attn_tile, ssd_tile, tail_tile):
    b, l, _ = x.shape
    m = b * l
    new_kv, new_h, new_conv = [], [], []
    for i in range(DEPTH):
        j = i // 2
        xf = x.reshape(m, D_MODEL)
        if i % 2 == 0:
            q, kv, cq = _norm_matmul(xf, p["norm_mix_pre"], i, [(p["w_in_attn"], ATTN_SEGS, (BF16, F32, BF16))], j, tm)
            kv = kv.reshape(b, l, 2 * KV_WIDTH)
            a = _attention(q.reshape(b, l, Q_WIDTH), kv, swa_prev, p["attn_bias"], j, *attn_tile)
            new_kv.append(kv)
        else:
            z, xbc, cq, dt = _norm_matmul(xf, p["norm_mix_pre"], i,
                                          [(p["w_in_ssm"], SSM_MAIN_SEGS, (F32, F32)),
                                           (p["w_in_side"], SSM_SIDE_SEGS, (BF16, F32))], j, tm)
            xbc = xbc.reshape(b, l, CONV_DIM)
            a, h_last = _ssd(z.reshape(b, l, SSM_INNER), xbc, dt.reshape(b, l, LANES), conv_hist, ssm_h0, p, j,
                             ssd_tile)
            new_h.append(h_last)
            new_conv.append(xbc[:, l - (CONV_W - 1):])
        x = _tail(a, cq.reshape(b, l, MEM_WIDTH), mem_k, mem_v, x, p, i, *tail_tile)
    return x, new_kv, jnp.stack(new_h), jnp.stack(new_conv)


def kernel(x_prompt, x_sample, mem_prompt, cache_swa_k, cache_swa_v, state_ssm, state_conv, cache_mem_k, cache_mem_v,
           norm_mix_pre, norm_mix_post, norm_ffn_pre, norm_ffn_post, w_in_attn, attn_sinks, w_in_ssm, conv_w, conv_b,
           dt_bias, a_log, d_skip, ssm_norm, mem_norm, w_mem_kv, w_out, w_up, w_down):
    p = _prep_weights(norm_mix_pre, norm_mix_post, norm_ffn_pre, norm_ffn_post, w_in_attn, attn_sinks, w_in_ssm,
                      conv_w, conv_b, dt_bias, a_log, d_skip, ssm_norm, w_out, w_up, w_down)
    bp, lp, _ = x_prompt.shape
    bs, ls, _ = x_sample.shape
    n_attn = w_in_attn.shape[0]
    n_ssm = w_in_ssm.shape[0]

    mem_rows = N_MEM * MEM_HEADS
    prompt_mem_k, prompt_mem_v = _mem_kv(mem_prompt.reshape(bp * N_MEM, D_MODEL),
                                         mem_norm.astype(F32).reshape(DEPTH, 1, D_MODEL), w_mem_kv.astype(BF16), 512)

    y_prompt, kv_p, prompt_ssm, prompt_conv = _trunk(
        x_prompt, prompt_mem_k.reshape(DEPTH, bp, mem_rows, MEM_HEAD_DIM),
        prompt_mem_v.reshape(DEPTH, bp, mem_rows, MEM_HEAD_DIM), None,
        jnp.zeros((n_ssm, bp, SSM_HEADS, SSM_HEAD_DIM, D_STATE), F32), jnp.zeros((n_ssm, bp, 8, CONV_DIM), F32), p,
        tm=512, attn_tile=(1, 256), ssd_tile=128, tail_tile=(1, 512))

    keep = cache_swa_k.shape[2]
    swa_prev = jnp.concatenate([cache_swa_k.reshape(n_attn, bs, keep, KV_WIDTH),
                                cache_swa_v.reshape(n_attn, bs, keep, KV_WIDTH)], axis=-1)
    hist = jnp.pad(state_conv, ((0, 0), (0, 0), (8 - (CONV_W - 1), 0), (0, 0)))
    y_sample, kv_s, sample_ssm, sample_conv = _trunk(
        x_sample, cache_mem_k.reshape(DEPTH, bs, mem_rows, MEM_HEAD_DIM),
        cache_mem_v.reshape(DEPTH, bs, mem_rows, MEM_HEAD_DIM), swa_prev, state_ssm, hist, p,
        tm=512, attn_tile=(4, ls), ssd_tile=ls, tail_tile=(8, ls))

    def split_kv(kvs, rows):
        kv = jnp.stack([kv[:, -rows:] for kv in kvs])
        shape = kv.shape[:3] + (N_KV_HEADS, HEAD_DIM)
        return kv[..., :KV_WIDTH].reshape(shape), kv[..., KV_WIDTH:].reshape(shape)

    prompt_swa_k, prompt_swa_v = split_kv(kv_p, min(WINDOW, lp))
    sample_swa_k, sample_swa_v = split_kv(kv_s, ls)
    mem_shape = (DEPTH, bp, N_MEM, MEM_HEADS, MEM_HEAD_DIM)
    return (y_prompt, y_sample, prompt_swa_k, prompt_swa_v, prompt_ssm, prompt_conv,
            prompt_mem_k.reshape(mem_shape), prompt_mem_v.reshape(mem_shape),
            sample_swa_k, sample_swa_v, sample_ssm, sample_conv)
```

```python
import functools
import math

import numpy as np
import jax
import jax.numpy as jnp
from jax import lax
from jax.experimental import pallas as pl
from jax.experimental.pallas import tpu as pltpu

F32 = jnp.float32
BF16 = jnp.bfloat16

D_MODEL = 1024
DEPTH = 4
CHUNK = 64
N_HEADS = 24
N_KV_HEADS = 3
HEAD_DIM = 64
WINDOW = 128
Q_WIDTH = N_HEADS * HEAD_DIM
KV_WIDTH = N_KV_HEADS * HEAD_DIM
N_MEM = 256
MEM_HEADS = 4
MEM_HEAD_DIM = 128
MEM_WIDTH = MEM_HEADS * MEM_HEAD_DIM
SSM_INNER = 1536
SSM_HEAD_DIM = 64
SSM_HEADS = SSM_INNER // SSM_HEAD_DIM
SSM_GROUPS = 8
SSM_GROUP_WIDTH = SSM_INNER // SSM_GROUPS
D_STATE = 128
CONV_W = 4
CONV_DIM = SSM_INNER + 2 * SSM_GROUPS * D_STATE
D_FF = 4 * D_MODEL
EPS = 1e-6

LANES = 128
HEAD_PAIRS = SSM_HEADS // 2
NEG = -1e30
VMEM_LIMIT = 56 * 1024 * 1024

NT_DIMS = (((1,), (1,)), ((), ()))


def _rms(x, w):
    return x * lax.rsqrt(jnp.mean(x * x, axis=-1, keepdims=True) + EPS) * w


def _sigmoid(x):
    return 1.0 / (1.0 + jnp.exp(-x))


def _split3(x):
    a = x.astype(BF16)
    r = x - a.astype(F32)
    b = r.astype(BF16)
    c = (r - b.astype(F32)).astype(BF16)
    return a, b, c


def _resident(shape):
    return pl.BlockSpec(shape, lambda *_: (0,) * len(shape), pipeline_mode=pl.Buffered(1))


def _layer(stacked, layer):
    shape = stacked.shape[1:]
    return pl.BlockSpec((None,) + shape, lambda *_: (layer,) + (0,) * len(shape), pipeline_mode=pl.Buffered(1))


def _norm_matmul_kernel(x_ref, nw_ref, *refs, plan):
    w_refs, o_refs = refs[:len(plan)], list(refs[len(plan):])
    h = _rms(x_ref[...], nw_ref[...]).astype(BF16)
    for w_ref, segs in zip(w_refs, plan):
        for start, width in segs:
            o_ref = o_refs.pop(0)
            for c0 in range(0, width, 512):
                cw = min(512, width - c0)
                o_ref[:, c0:c0 + cw] = jnp.dot(
                    h, w_ref[:, start + c0:start + c0 + cw], preferred_element_type=F32).astype(o_ref.dtype)


def _norm_matmul(x, nw, nw_layer, weights, w_layer, tm):
    m, k = x.shape
    plan = tuple(segs for _, segs, _ in weights)
    widths = [wd for _, segs, _ in weights for _, wd in segs]
    dtypes = [dt for _, _, dts in weights for dt in dts]
    return pl.pallas_call(
        functools.partial(_norm_matmul_kernel, plan=plan),
        grid=(m // tm,),
        in_specs=([pl.BlockSpec((tm, k), lambda i: (i, 0)), _layer(nw, nw_layer)]
                  + [_layer(w, w_layer) for w, _, _ in weights]),
        out_specs=[pl.BlockSpec((tm, wd), lambda i: (i, 0)) for wd in widths],
        out_shape=[jax.ShapeDtypeStruct((m, wd), dt) for wd, dt in zip(widths, dtypes)],
        compiler_params=pltpu.CompilerParams(dimension_semantics=("parallel",), vmem_limit_bytes=VMEM_LIMIT),
        name="norm_matmul",
    )(x, nw, *[w for w, _, _ in weights])


def _mem_kv_kernel(x_ref, nw_ref, w_ref, k_ref, v_ref):
    tm = x_ref.shape[0]
    h = _rms(x_ref[...], nw_ref[...]).astype(BF16)
    for o_ref, c0 in ((k_ref, 0), (v_ref, MEM_WIDTH)):
        r = jnp.dot(h, w_ref[:, c0:c0 + MEM_WIDTH], preferred_element_type=F32)
        for hd in range(MEM_HEADS):
            o_ref[pl.ds(hd, tm, stride=MEM_HEADS), :] = r[:, hd * MEM_HEAD_DIM:(hd + 1) * MEM_HEAD_DIM]


def _mem_kv(mem, nw, w, tm):
    m, k = mem.shape
    out = jax.ShapeDtypeStruct((DEPTH, m * MEM_HEADS, MEM_HEAD_DIM), F32)
    out_spec = pl.BlockSpec((None, tm * MEM_HEADS, MEM_HEAD_DIM), lambda l, i: (l, i, 0))
    return pl.pallas_call(
        _mem_kv_kernel,
        grid=(DEPTH, m // tm),
        in_specs=[pl.BlockSpec((tm, k), lambda l, i: (i, 0)),
                  pl.BlockSpec((None, 1, k), lambda l, i: (l, 0, 0)),
                  pl.BlockSpec((None, k, 2 * MEM_WIDTH), lambda l, i: (l, 0, 0))],
        out_specs=[out_spec, out_spec],
        out_shape=[out, out],
        compiler_params=pltpu.CompilerParams(dimension_semantics=("parallel", "parallel"),
                                             vmem_limit_bytes=VMEM_LIMIT),
        name="mem_kv",
    )(mem, nw, w)


def _attn_bias_table(sinks):
    slopes = jnp.asarray(2.0 ** (-8.0 * np.arange(1, N_HEADS + 1) / N_HEADS), dtype=F32).reshape(N_KV_HEADS, 4, 1, 2, 1)
    span = WINDOW + CHUNK
    i = np.arange(CHUNK)[:, None]
    s = np.arange(2 * LANES)[None, :]
    dist = jnp.asarray(np.abs(WINDOW + i - s), dtype=F32)[None, None, :, None, :]
    slot = jnp.asarray(s)[None, None, :, None, :]
    bias = -slopes * dist
    bias = jnp.where(slot < span, bias, NEG)
    bias = jnp.where(slot == span, sinks.astype(F32).reshape(N_KV_HEADS, 4, 1, 2, 1), bias)
    return bias.reshape(N_KV_HEADS, 4 * CHUNK, 4 * LANES)


def _attn_kernel(q_ref, kvc_ref, kvp_ref, bias_ref, o_ref, kvbuf, *, nb, T, mask_first):
    first = pl.program_id(1) == 0
    lo = lax.broadcasted_iota(jnp.int32, (1, LANES), 1) < HEAD_DIM
    slot = lax.broadcasted_iota(jnp.int32, (1, 4 * LANES), 1) & (2 * LANES - 1)
    zpad = jnp.zeros((CHUNK, LANES), BF16)
    span = WINDOW + CHUNK
    for n in range(nb):
        kvbuf[0:WINDOW, :] = kvp_ref[n]
        kvbuf[WINDOW:WINDOW + T, :] = kvc_ref[n]
        for j in range(T // CHUNK):
            win = kvbuf[j * CHUNK:j * CHUNK + span, :]
            blocks = [win[:, m * LANES:(m + 1) * LANES] for m in range(3)]
            rolled = [pltpu.roll(b, HEAD_DIM, 1) for b in blocks]

            def halves(idx):
                m, half = divmod(idx, 2)
                in_lo, in_hi = (blocks[m], rolled[m]) if half == 0 else (rolled[m], blocks[m])
                return jnp.where(lo, in_lo, 0.0).astype(BF16), jnp.where(lo, 0.0, in_hi).astype(BF16)

            for kvh in range(N_KV_HEADS):
                klo, khi = halves(kvh)
                vlo, vhi = halves(N_KV_HEADS + kvh)
                kbd = jnp.concatenate([klo, zpad, khi, zpad], axis=0)
                c0 = kvh * 4 * LANES
                qs = jnp.concatenate(
                    [q_ref[n, j * CHUNK:(j + 1) * CHUNK, c0 + p * LANES:c0 + (p + 1) * LANES] for p in range(4)],
                    axis=0)
                s = lax.dot_general(qs * (1.0 / math.sqrt(HEAD_DIM)), kbd, NT_DIMS, preferred_element_type=F32)
                s = s + bias_ref[kvh]
                if mask_first and j < WINDOW // CHUNK:
                    s = jnp.where(slot < jnp.where(first, WINDOW - j * CHUNK, 0), NEG, s)
                ps, inv = [], []
                for e in range(2):
                    se = s[:, e * 2 * LANES:(e + 1) * 2 * LANES]
                    p = jnp.exp(se - jnp.max(se, axis=-1, keepdims=True))
                    inv.append(1.0 / jnp.sum(p, axis=-1, keepdims=True))
                    ps.append(p.astype(BF16))
                o = jnp.dot(jnp.concatenate(ps, axis=1), jnp.concatenate([vlo, zpad, vhi, zpad], axis=0),
                            preferred_element_type=F32)
                o = o * jnp.where(lo, inv[0], inv[1])
                for p in range(4):
                    o_ref[n, j * CHUNK:(j + 1) * CHUNK, c0 + p * LANES:c0 + (p + 1) * LANES] = (
                        o[p * CHUNK:(p + 1) * CHUNK].astype(o_ref.dtype))


def _attention(q, kv, kv_prev, bias, layer, nb, T):
    b, l, _ = q.shape
    mask_first = kv_prev is None
    if mask_first:
        prev_arr = kv
        prev_spec = pl.BlockSpec((nb, WINDOW, 2 * KV_WIDTH),
                                 lambda bi, i: (bi, jnp.maximum(i * (T // WINDOW) - 1, 0), 0))
    else:
        prev_arr = kv_prev
        prev_spec = pl.BlockSpec((None, nb, WINDOW, 2 * KV_WIDTH), lambda bi, i: (layer, bi, 0, 0))
    return pl.pallas_call(
        functools.partial(_attn_kernel, nb=nb, T=T, mask_first=mask_first),
        grid=(b // nb, l // T),
        in_specs=[pl.BlockSpec((nb, T, Q_WIDTH), lambda bi, i: (bi, i, 0)),
                  pl.BlockSpec((nb, T, 2 * KV_WIDTH), lambda bi, i: (bi, i, 0)),
                  prev_spec,
                  _layer(bias, layer)],
        out_specs=pl.BlockSpec((nb, T, Q_WIDTH), lambda bi, i: (bi, i, 0)),
        out_shape=jax.ShapeDtypeStruct((b, l, Q_WIDTH), BF16),
        scratch_shapes=[pltpu.VMEM((WINDOW + T, 2 * KV_WIDTH), F32)],
        compiler_params=pltpu.CompilerParams(dimension_semantics=("parallel", "parallel"),
                                             vmem_limit_bytes=VMEM_LIMIT),
        name="swa_attention",
    )(q, kv, prev_arr, bias)


def _ssd_kernel(z_ref, xbc_ref, dt_ref, hist_ref, h0_ref, cw_ref, cb_ref, dtb_ref, alog_ref, dsk_ref, nw_ref,
                e64_ref, y_ref, hout_ref, cbuf, ubuf, ybuf, dtx_s, acs_s, st, *, T):
    i = pl.program_id(1)

    @pl.when(i == 0)
    def _init():
        cbuf[0:8, :] = hist_ref[0]
        for hp in range(HEAD_PAIRS):
            st[hp] = h0_ref[0, 2 * hp:2 * hp + 2].reshape(2 * SSM_HEAD_DIM, D_STATE).T

    cbuf[8:8 + T, :] = xbc_ref[0]

    row = lax.broadcasted_iota(jnp.int32, (CHUNK, LANES), 0)
    pos = lax.broadcasted_iota(jnp.int32, (CHUNK, LANES), 1) & (CHUNK - 1)
    diag = pos == row
    causal = pos <= row
    lo = lax.broadcasted_iota(jnp.int32, (1, LANES), 1) < SSM_HEAD_DIM
    tri = (lax.broadcasted_iota(jnp.int32, (CHUNK, CHUNK), 1)
           <= lax.broadcasted_iota(jnp.int32, (CHUNK, CHUNK), 0)).astype(F32).astype(BF16)
    a_neg = -jnp.exp(alog_ref[...])
    b_off = SSM_INNER
    c_off = SSM_INNER + SSM_GROUPS * D_STATE

    for j in range(T // CHUNK):
        r0 = 8 + j * CHUNK
        rows = slice(j * CHUNK, (j + 1) * CHUNK)
        for c0 in range(0, CONV_DIM, 512):
            cols = slice(c0, c0 + 512)
            xw = cbuf[r0 - 8:r0 + CHUNK, cols]
            xw1 = pltpu.roll(xw, 1, 0)
            near = xw * cw_ref[3:4, cols] + xw1 * cw_ref[2:3, cols]
            far = pltpu.roll(xw * cw_ref[1:2, cols] + xw1 * cw_ref[0:1, cols], 2, 0)
            acc = (near + far)[8:] + cb_ref[:, cols]
            ubuf[:, cols] = acc * _sigmoid(acc)

        dt_in = dt_ref[0, rows, :] + dtb_ref[...]
        dtv = jnp.maximum(dt_in, 0.0) + jnp.log1p(jnp.exp(-jnp.abs(dt_in)))
        cum3 = jnp.dot(tri, jnp.concatenate(_split3(dtv * a_neg), axis=1), preferred_element_type=F32)
        cum = cum3[:, 2 * LANES:] + cum3[:, LANES:2 * LANES] + cum3[:, :LANES]
        big = jnp.dot(jnp.concatenate(_split3(dtv) + _split3(cum), axis=0), e64_ref[...],
                      preferred_element_type=F32)
        dtx_s[...] = big[128:192] + big[64:128] + big[0:64]
        acs_s[...] = big[320:384] + big[256:320] + big[192:256]

        bb, bt, cb, cbd = [], [], [], []
        for g in range(SSM_GROUPS):
            bg = ubuf[:, b_off + g * D_STATE:b_off + (g + 1) * D_STATE]
            bb.append(bg.astype(BF16))
            bt.append(bg.T.astype(BF16))
            cb.append(ubuf[:, c_off + g * D_STATE:c_off + (g + 1) * D_STATE].astype(BF16))
            cbd.append(lax.dot_general(cb[g], jnp.concatenate([bb[g], bb[g]], axis=0), NT_DIMS,
                                       preferred_element_type=F32))

        for hp in range(HEAD_PAIRS):
            sl = slice(hp * LANES, (hp + 1) * LANES)
            g0, g1 = (2 * hp) // 3, (2 * hp + 1) // 3
            acs = acs_s[:, sl]
            acs_key = jnp.sum(jnp.where(diag, acs, 0.0), axis=0, keepdims=True)
            lmat = jnp.exp(jnp.where(causal, acs - acs_key, NEG))
            eacs = jnp.exp(acs)
            acs_last = acs[CHUNK - 1:CHUNK, :]
            xp = ubuf[:, sl]
            xdt = xp * dtx_s[:, sl]
            xlo = jnp.where(lo, xdt, 0.0)
            xhi = jnp.where(lo, 0.0, xdt)

            cbp = cbd[g0] if g0 == g1 else jnp.where(lo, cbd[g0], cbd[g1])
            xbd = jnp.concatenate([xlo, xhi], axis=0).astype(BF16)
            yd = jnp.dot((cbp * lmat).astype(BF16), xbd, preferred_element_type=F32)

            ps = st[hp]
            psb = ps.astype(BF16)
            yo = jnp.dot(cb[g0], psb, preferred_element_type=F32)
            if g0 != g1:
                yo = jnp.where(lo, yo, jnp.dot(cb[g1], psb, preferred_element_type=F32))
            ybuf[:, sl] = yd + yo * eacs + dsk_ref[:, sl] * xp

            wx = jnp.exp(acs_last - acs)
            if g0 == g1:
                new = jnp.dot(bt[g0], (xdt * wx).astype(BF16), preferred_element_type=F32)
            else:
                new = (jnp.dot(bt[g0], (xlo * wx).astype(BF16), preferred_element_type=F32)
                       + jnp.dot(bt[g1], (xhi * wx).astype(BF16), preferred_element_type=F32))
            st[hp] = ps * eacs[CHUNK - 1:CHUNK, :] + new

        zc = z_ref[0, rows, :]
        yg = ybuf[...] * (zc * _sigmoid(zc))
        y2 = yg * yg
        rs = []
        for g in range(SSM_GROUPS):
            g_lo, g_hi = g * SSM_GROUP_WIDTH, (g + 1) * SSM_GROUP_WIDTH
            w0 = (g_lo // LANES) * LANES
            w1 = -(-g_hi // LANES) * LANES
            lane = lax.broadcasted_iota(jnp.int32, (1, w1 - w0), 1) + w0
            ss = jnp.sum(jnp.where(lane < g_lo, 0.0, jnp.where(lane < g_hi, y2[:, w0:w1], 0.0)),
                         axis=-1, keepdims=True)
            rs.append(lax.rsqrt(ss / SSM_GROUP_WIDTH + EPS))
        lane = lax.broadcasted_iota(jnp.int32, (1, LANES), 1)
        for v in range(SSM_INNER // LANES):
            ga, gb = (v * LANES) // SSM_GROUP_WIDTH, (v * LANES + LANES - 1) // SSM_GROUP_WIDTH
            sc = rs[ga] if ga == gb else jnp.where(lane < gb * SSM_GROUP_WIDTH - v * LANES, rs[ga], rs[gb])
            vs = slice(v * LANES, (v + 1) * LANES)
            y_ref[0, rows, vs] = (yg[:, vs] * sc * nw_ref[:, vs]).astype(y_ref.dtype)

    cbuf[0:8, :] = cbuf[T:T + 8, :]

    @pl.when(i == pl.num_programs(1) - 1)
    def _fin():
        for hp in range(HEAD_PAIRS):
            hout_ref[0, 2 * hp:2 * hp + 2] = st[hp].T.reshape(2, SSM_HEAD_DIM, D_STATE)


SSM_PARAMS = ("conv_w", "conv_b", "dt_bias", "a_log", "d_skip", "ssm_norm")


def _ssd(z, xbc, dt, hist, h0, p, layer, T):
    b, l, _ = z.shape
    e64 = jnp.asarray(np.arange(LANES)[:, None] == (np.arange(SSM_INNER)[None, :] // SSM_HEAD_DIM), dtype=BF16)
    tok = lambda w: pl.BlockSpec((1, T, w), lambda bi, i: (bi, i, 0))
    return pl.pallas_call(
        functools.partial(_ssd_kernel, T=T),
        grid=(b, l // T),
        in_specs=[tok(SSM_INNER), tok(CONV_DIM), tok(LANES),
                  pl.BlockSpec((None, 1, 8, CONV_DIM), lambda bi, i: (layer, bi, 0, 0)),
                  pl.BlockSpec((None, 1, SSM_HEADS, SSM_HEAD_DIM, D_STATE), lambda bi, i: (layer, bi, 0, 0, 0)),
                  *[_layer(p[k], layer) for k in SSM_PARAMS],
                  _resident((LANES, SSM_INNER))],
        out_specs=[tok(SSM_INNER),
                   pl.BlockSpec((1, SSM_HEADS, SSM_HEAD_DIM, D_STATE), lambda bi, i: (bi, 0, 0, 0))],
        out_shape=[jax.ShapeDtypeStruct((b, l, SSM_INNER), BF16),
                   jax.ShapeDtypeStruct((b, SSM_HEADS, SSM_HEAD_DIM, D_STATE), F32)],
        scratch_shapes=[pltpu.VMEM((8 + T, CONV_DIM), F32), pltpu.VMEM((CHUNK, CONV_DIM), F32),
                        pltpu.VMEM((CHUNK, SSM_INNER), F32), pltpu.VMEM((CHUNK, SSM_INNER), F32),
                        pltpu.VMEM((CHUNK, SSM_INNER), F32), pltpu.VMEM((HEAD_PAIRS, D_STATE, LANES), F32)],
        compiler_params=pltpu.CompilerParams(dimension_semantics=("parallel", "arbitrary"),
                                             vmem_limit_bytes=VMEM_LIMIT),
        name="conv_ssd",
    )(z, xbc, dt, hist, h0, *[p[k] for k in SSM_PARAMS], e64)


def _tail_kernel(a_ref, cq_ref, mk_ref, mv_ref, x_ref, wout_ref, npost_ref, nfpre_ref, wup_ref, wdn_ref, nfpost_ref,
                 o_ref, *, nb, T):
    m = nb * T
    cs = []
    for n in range(nb):
        outs = []
        for h in range(MEM_HEADS):
            hs = slice(h * MEM_HEAD_DIM, (h + 1) * MEM_HEAD_DIM)
            head_rows = pl.ds(h, N_MEM, stride=MEM_HEADS)
            mk = mk_ref[n, head_rows, :].astype(BF16)
            mv = mv_ref[n, head_rows, :].astype(BF16)
            s = lax.dot_general(cq_ref[n, :, hs], mk, NT_DIMS, preferred_element_type=F32)
            s = s * (1.0 / math.sqrt(MEM_HEAD_DIM))
            p = jnp.exp(s - jnp.max(s, axis=-1, keepdims=True))
            den = jnp.sum(p, axis=-1, keepdims=True)
            o = jnp.dot(p.astype(BF16), mv, preferred_element_type=F32) * (1.0 / den)
            outs.append(o.astype(BF16))
        cs.append(jnp.concatenate(outs, axis=1))
    c = cs[0] if nb == 1 else jnp.concatenate(cs, axis=0)
    a = a_ref[...].reshape(m, Q_WIDTH)
    mix = (jnp.dot(a, wout_ref[0:Q_WIDTH, :], preferred_element_type=F32)
           + jnp.dot(c, wout_ref[Q_WIDTH:, :], preferred_element_type=F32))
    x1 = x_ref[...].reshape(m, D_MODEL) + _rms(mix, npost_ref[...])
    h2 = _rms(x1, nfpre_ref[...]).astype(BF16)
    acc = None
    for f0 in range(0, D_FF, 1024):
        u = jnp.maximum(jnp.dot(h2, wup_ref[:, f0:f0 + 1024], preferred_element_type=F32), 0.0)
        d = jnp.dot((u * u).astype(BF16), wdn_ref[f0:f0 + 1024, :], preferred_element_type=F32)
        acc = d if acc is None else acc + d
    o_ref[...] = (x1 + _rms(acc, nfpost_ref[...])).reshape(nb, T, D_MODEL)


TAIL_PARAMS = ("w_out", "norm_mix_post", "norm_ffn_pre", "w_up", "w_down", "norm_ffn_post")


def _tail(a, cq, mk, mv, x, p, layer, nb, T):
    b, l, _ = x.shape
    tok = lambda w: pl.BlockSpec((nb, T, w), lambda bi, i: (bi, i, 0))
    mem = pl.BlockSpec((None, nb, N_MEM * MEM_HEADS, MEM_HEAD_DIM), lambda bi, i: (layer, bi, 0, 0))
    return pl.pallas_call(
        functools.partial(_tail_kernel, nb=nb, T=T),
        grid=(b // nb, l // T),
        in_specs=[tok(Q_WIDTH), tok(MEM_WIDTH), mem, mem, tok(D_MODEL),
                  *[_layer(p[k], layer) for k in TAIL_PARAMS]],
        out_specs=tok(D_MODEL),
        out_shape=jax.ShapeDtypeStruct((b, l, D_MODEL), F32),
        compiler_params=pltpu.CompilerParams(dimension_semantics=("parallel", "parallel"),
                                             vmem_limit_bytes=VMEM_LIMIT),
        name="xattn_outproj_mlp",
    )(a, cq, mk, mv, x, *[p[k] for k in TAIL_PARAMS])


ATTN_SEGS = ((0, Q_WIDTH), (Q_WIDTH, 2 * KV_WIDTH), (Q_WIDTH + 2 * KV_WIDTH, MEM_WIDTH))
SSM_MAIN_SEGS = ((0, SSM_INNER), (SSM_INNER, CONV_DIM))
SSM_SIDE_SEGS = ((0, MEM_WIDTH), (MEM_WIDTH, LANES))


def _prep_weights(norm_mix_pre, norm_mix_post, norm_ffn_pre, norm_ffn_post, w_in_attn, attn_sinks, w_in_ssm, conv_w,
                  conv_b, dt_bias, a_log, d_skip, ssm_norm, w_out, w_up, w_down):
    dt_lo = SSM_INNER + CONV_DIM
    dt_hi = dt_lo + SSM_HEADS
    pad = jnp.zeros((w_in_ssm.shape[0], D_MODEL, LANES - SSM_HEADS), F32)
    w_side = jnp.concatenate([w_in_ssm[..., dt_hi:], w_in_ssm[..., dt_lo:dt_hi], pad], axis=-1)
    vec = lambda v: v.astype(F32).reshape(v.shape[0], 1, -1)
    lane_pad = lambda v: vec(jnp.pad(v, ((0, 0), (0, LANES - SSM_HEADS))))
    return dict(
        norm_mix_pre=vec(norm_mix_pre), norm_mix_post=vec(norm_mix_post), norm_ffn_pre=vec(norm_ffn_pre),
        norm_ffn_post=vec(norm_ffn_post),
        w_in_attn=w_in_attn.astype(BF16), w_in_ssm=w_in_ssm.astype(BF16), w_in_side=w_side.astype(BF16),
        attn_bias=jnp.stack([_attn_bias_table(attn_sinks[j]) for j in range(attn_sinks.shape[0])]),
        conv_w=conv_w.astype(F32), conv_b=vec(conv_b), dt_bias=lane_pad(dt_bias), a_log=lane_pad(a_log),
        d_skip=vec(jnp.repeat(d_skip, SSM_HEAD_DIM, axis=-1)), ssm_norm=vec(ssm_norm),
        w_out=w_out.astype(BF16), w_up=w_up.astype(BF16), w_down=w_down.astype(BF16))


def _trunk(x, mem_k, mem_v, swa_prev, ssm_h0, conv_hist, p, *, tm, attn_tile, ssd_tile, tail_tile):
    b, l, _ = x.shape
    m = b * l
    new_kv, new_h, new_conv = [], [], []
    for i in range(DEPTH):
        j = i // 2
        xf = x.reshape(m, D_MODEL)
        if i % 2 == 0:
            q, kv, cq = _norm_matmul(xf, p["norm_mix_pre"], i, [(p["w_in_attn"], ATTN_SEGS, (BF16, F32, BF16))], j, tm)
            kv = kv.reshape(b, l, 2 * KV_WIDTH)
            a = _attention(q.reshape(b, l, Q_WIDTH), kv, swa_prev, p["attn_bias"], j, *attn_tile)
            new_kv.append(kv)
        else:
            z, xbc, cq, dt = _norm_matmul(xf, p["norm_mix_pre"], i,
                                          [(p["w_in_ssm"], SSM_MAIN_SEGS, (F32, F32)),
                                           (p["w_in_side"], SSM_SIDE_SEGS, (BF16, F32))], j, tm)
            xbc = xbc.reshape(b, l, CONV_DIM)
            a, h_last = _ssd(z.reshape(b, l, SSM_INNER), xbc, dt.reshape(b, l, LANES), conv_hist, ssm_h0, p, j,
                             ssd_tile)
            new_h.append(h_last)
            new_conv.append(xbc[:, l - (CONV_W - 1):])
        x = _tail(a, cq.reshape(b, l, MEM_WIDTH), mem_k, mem_v, x, p, i, *tail_tile)
    return x, new_kv, jnp.stack(new_h), jnp.stack(new_conv)


def kernel(x_prompt, x_sample, mem_prompt, cache_swa_k, cache_swa_v, state_ssm, state_conv, cache_mem_k, cache_mem_v,
           norm_mix_pre, norm_mix_post, norm_ffn_pre, norm_ffn_post, w_in_attn, attn_sinks, w_in_ssm, conv_w, conv_b,
           dt_bias, a_log, d_skip, ssm_norm, mem_norm, w_mem_kv, w_out, w_up, w_down):
    p = _prep_weights(norm_mix_pre, norm_mix_post, norm_ffn_pre, norm_ffn_post, w_in_attn, attn_sinks, w_in_ssm,
                      conv_w, conv_b, dt_bias, a_log, d_skip, ssm_norm, w_out, w_up, w_down)
    bp, lp, _ = x_prompt.shape
    bs, ls, _ = x_sample.shape
    n_attn = w_in_attn.shape[0]
    n_ssm = w_in_ssm.shape[0]

    mem_rows = N_MEM * MEM_HEADS
    prompt_mem_k, prompt_mem_v = _mem_kv(mem_prompt.reshape(bp * N_MEM, D_MODEL),
                                         mem_norm.astype(F32).reshape(DEPTH, 1, D_MODEL), w_mem_kv.astype(BF16), 512)

    y_prompt, kv_p, prompt_ssm, prompt_conv = _trunk(
        x_prompt, prompt_mem_k.reshape(DEPTH, bp, mem_rows, MEM_HEAD_DIM),
        prompt_mem_v.reshape(DEPTH, bp, mem_rows, MEM_HEAD_DIM), None,
        jnp.zeros((n_ssm, bp, SSM_HEADS, SSM_HEAD_DIM, D_STATE), F32), jnp.zeros((n_ssm, bp, 8, CONV_DIM), F32), p,
        tm=512, attn_tile=(1, 256), ssd_tile=256, tail_tile=(1, 512))

    keep = cache_swa_k.shape[2]
    swa_prev = jnp.concatenate([cache_swa_k.reshape(n_attn, bs, keep, KV_WIDTH),
                                cache_swa_v.reshape(n_attn, bs, keep, KV_WIDTH)], axis=-1)
    hist = jnp.pad(state_conv, ((0, 0), (0, 0), (8 - (CONV_W - 1), 0), (0, 0)))
    y_sample, kv_s, sample_ssm, sample_conv = _trunk(
        x_sample, cache_mem_k.reshape(DEPTH, bs, mem_rows, MEM_HEAD_DIM),
        cache_mem_v.reshape(DEPTH, bs, mem_rows, MEM_HEAD_DIM), swa_prev, state_ssm, hist, p,
        tm=512, attn_tile=(4, ls), ssd_tile=ls, tail_tile=(8, ls))

    def split_kv(kvs, rows):
        kv = jnp.stack([kv[:, -rows:] for kv in kvs])
        shape = kv.shape[:3] + (N_KV_HEADS, HEAD_DIM)
        return kv[..., :KV_WIDTH].reshape(shape), kv[..., KV_WIDTH:].reshape(shape)

    prompt_swa_k, prompt_swa_v = split_kv(kv_p, min(WINDOW, lp))
    sample_swa_k, sample_swa_v = split_kv(kv_s, ls)
    mem_shape = (DEPTH, bp, N_MEM, MEM_HEADS, MEM_HEAD_DIM)
    return (y_prompt, y_sample, prompt_swa_k, prompt_swa_v, prompt_ssm, prompt_conv,
            prompt_mem_k.reshape(mem_shape), prompt_mem_v.reshape(mem_shape),
            sample_swa_k, sample_swa_v, sample_ssm, sample_conv)
```

```python
import functools
import math

import numpy as np
import jax
import jax.numpy as jnp
from jax import lax
from jax.experimental import pallas as pl
from jax.experimental.pallas import tpu as pltpu

F32 = jnp.float32
BF16 = jnp.bfloat16

D_MODEL = 1024
DEPTH = 4
CHUNK = 64
N_HEADS = 24
N_KV_HEADS = 3
HEAD_DIM = 64
WINDOW = 128
Q_WIDTH = N_HEADS * HEAD_DIM
KV_WIDTH = N_KV_HEADS * HEAD_DIM
N_MEM = 256
MEM_HEADS = 4
MEM_HEAD_DIM = 128
MEM_WIDTH = MEM_HEADS * MEM_HEAD_DIM
SSM_INNER = 1536
SSM_HEAD_DIM = 64
SSM_HEADS = SSM_INNER // SSM_HEAD_DIM
SSM_GROUPS = 8
SSM_GROUP_WIDTH = SSM_INNER // SSM_GROUPS
D_STATE = 128
CONV_W = 4
CONV_DIM = SSM_INNER + 2 * SSM_GROUPS * D_STATE
D_FF = 4 * D_MODEL
EPS = 1e-6

LANES = 128
HEAD_PAIRS = SSM_HEADS // 2
NEG = -1e30
VMEM_LIMIT = 56 * 1024 * 1024

NT_DIMS = (((1,), (1,)), ((), ()))


def _rms(x, w):
    return x * lax.rsqrt(jnp.mean(x * x, axis=-1, keepdims=True) + EPS) * w


def _sigmoid(x):
    return 1.0 / (1.0 + jnp.exp(-x))


def _split3(x):
    a = x.astype(BF16)
    r = x - a.astype(F32)
    b = r.astype(BF16)
    c = (r - b.astype(F32)).astype(BF16)
    return a, b, c


def _resident(shape):
    return pl.BlockSpec(shape, lambda *_: (0,) * len(shape), pipeline_mode=pl.Buffered(1))


def _layer(stacked, layer):
    shape = stacked.shape[1:]
    return pl.BlockSpec((None,) + shape, lambda *_: (layer,) + (0,) * len(shape), pipeline_mode=pl.Buffered(1))


def _norm_matmul_kernel(x_ref, nw_ref, *refs, plan):
    w_refs, o_refs = refs[:len(plan)], list(refs[len(plan):])
    h = _rms(x_ref[...], nw_ref[...]).astype(BF16)
    for w_ref, segs in zip(w_refs, plan):
        for start, width in segs:
            o_ref = o_refs.pop(0)
            for c0 in range(0, width, 512):
                cw = min(512, width - c0)
                o_ref[:, c0:c0 + cw] = jnp.dot(
                    h, w_ref[:, start + c0:start + c0 + cw].astype(BF16), preferred_element_type=F32).astype(o_ref.dtype)


def _norm_matmul(x, nw, nw_layer, weights, w_layer, tm):
    m, k = x.shape
    plan = tuple(segs for _, segs, _ in weights)
    widths = [wd for _, segs, _ in weights for _, wd in segs]
    dtypes = [dt for _, _, dts in weights for dt in dts]
    return pl.pallas_call(
        functools.partial(_norm_matmul_kernel, plan=plan),
        grid=(m // tm,),
        in_specs=([pl.BlockSpec((tm, k), lambda i: (i, 0)), _layer(nw, nw_layer)]
                  + [_layer(w, w_layer) if w.ndim == 3 else _resident(w.shape) for w, _, _ in weights]),
        out_specs=[pl.BlockSpec((tm, wd), lambda i: (i, 0)) for wd in widths],
        out_shape=[jax.ShapeDtypeStruct((m, wd), dt) for wd, dt in zip(widths, dtypes)],
        compiler_params=pltpu.CompilerParams(dimension_semantics=("parallel",), vmem_limit_bytes=VMEM_LIMIT),
        name="norm_matmul",
    )(x, nw, *[w for w, _, _ in weights])


def _mem_kv_kernel(x_ref, nw_ref, w_ref, k_ref, v_ref):
    tm = x_ref.shape[0]
    h = _rms(x_ref[...], nw_ref[...]).astype(BF16)
    for o_ref, c0 in ((k_ref, 0), (v_ref, MEM_WIDTH)):
        r = jnp.dot(h, w_ref[:, c0:c0 + MEM_WIDTH], preferred_element_type=F32)
        for hd in range(MEM_HEADS):
            o_ref[pl.ds(hd, tm, stride=MEM_HEADS), :] = r[:, hd * MEM_HEAD_DIM:(hd + 1) * MEM_HEAD_DIM]


def _mem_kv(mem, nw, w, tm):
    m, k = mem.shape
    out = jax.ShapeDtypeStruct((DEPTH, m * MEM_HEADS, MEM_HEAD_DIM), F32)
    out_spec = pl.BlockSpec((None, tm * MEM_HEADS, MEM_HEAD_DIM), lambda l, i: (l, i, 0))
    return pl.pallas_call(
        _mem_kv_kernel,
        grid=(DEPTH, m // tm),
        in_specs=[pl.BlockSpec((tm, k), lambda l, i: (i, 0)),
                  pl.BlockSpec((None, 1, k), lambda l, i: (l, 0, 0)),
                  pl.BlockSpec((None, k, 2 * MEM_WIDTH), lambda l, i: (l, 0, 0))],
        out_specs=[out_spec, out_spec],
        out_shape=[out, out],
        compiler_params=pltpu.CompilerParams(dimension_semantics=("parallel", "parallel"),
                                             vmem_limit_bytes=VMEM_LIMIT),
        name="mem_kv",
    )(mem, nw, w)


def _attn_bias_table(sinks):
    slopes = jnp.asarray(2.0 ** (-8.0 * np.arange(1, N_HEADS + 1) / N_HEADS), dtype=F32).reshape(N_KV_HEADS, 4, 1, 2, 1)
    span = WINDOW + CHUNK
    i = np.arange(CHUNK)[:, None]
    s = np.arange(2 * LANES)[None, :]
    dist = jnp.asarray(np.abs(WINDOW + i - s), dtype=F32)[None, None, :, None, :]
    slot = jnp.asarray(s)[None, None, :, None, :]
    bias = -slopes * dist
    bias = jnp.where(slot < span, bias, NEG)
    bias = jnp.where(slot == span, sinks.astype(F32).reshape(N_KV_HEADS, 4, 1, 2, 1), bias)
    return bias.reshape(N_KV_HEADS, 4 * CHUNK, 4 * LANES)


def _attn_kernel(q_ref, kvc_ref, kvp_ref, bias_ref, o_ref, kvbuf, *, nb, T, mask_first):
    first = pl.program_id(1) == 0
    lo = lax.broadcasted_iota(jnp.int32, (1, LANES), 1) < HEAD_DIM
    slot = lax.broadcasted_iota(jnp.int32, (1, 4 * LANES), 1) & (2 * LANES - 1)
    zpad = jnp.zeros((CHUNK, LANES), BF16)
    span = WINDOW + CHUNK
    for n in range(nb):
        kvbuf[0:WINDOW, :] = kvp_ref[n]
        kvbuf[WINDOW:WINDOW + T, :] = kvc_ref[n]
        for j in range(T // CHUNK):
            win = kvbuf[j * CHUNK:j * CHUNK + span, :]
            blocks = [win[:, m * LANES:(m + 1) * LANES] for m in range(3)]
            rolled = [pltpu.roll(b, HEAD_DIM, 1) for b in blocks]

            def halves(idx):
                m, half = divmod(idx, 2)
                in_lo, in_hi = (blocks[m], rolled[m]) if half == 0 else (rolled[m], blocks[m])
                return jnp.where(lo, in_lo, 0.0).astype(BF16), jnp.where(lo, 0.0, in_hi).astype(BF16)

            for kvh in range(N_KV_HEADS):
                klo, khi = halves(kvh)
                vlo, vhi = halves(N_KV_HEADS + kvh)
                kbd = jnp.concatenate([klo, zpad, khi, zpad], axis=0)
                c0 = kvh * 4 * LANES
                qs = jnp.concatenate(
                    [q_ref[n, j * CHUNK:(j + 1) * CHUNK, c0 + p * LANES:c0 + (p + 1) * LANES] for p in range(4)],
                    axis=0)
                s = lax.dot_general(qs * (1.0 / math.sqrt(HEAD_DIM)), kbd, NT_DIMS, preferred_element_type=F32)
                s = s + bias_ref[kvh]
                if mask_first and j < WINDOW // CHUNK:
                    s = jnp.where(slot < jnp.where(first, WINDOW - j * CHUNK, 0), NEG, s)
                ps, inv = [], []
                for e in range(2):
                    se = s[:, e * 2 * LANES:(e + 1) * 2 * LANES]
                    p = jnp.exp(se - jnp.max(se, axis=-1, keepdims=True))
                    inv.append(1.0 / jnp.sum(p, axis=-1, keepdims=True))
                    ps.append(p.astype(BF16))
                o = jnp.dot(jnp.concatenate(ps, axis=1), jnp.concatenate([vlo, zpad, vhi, zpad], axis=0),
                            preferred_element_type=F32)
                o = o * jnp.where(lo, inv[0], inv[1])
                for p in range(4):
                    o_ref[n, j * CHUNK:(j + 1) * CHUNK, c0 + p * LANES:c0 + (p + 1) * LANES] = (
                        o[p * CHUNK:(p + 1) * CHUNK].astype(o_ref.dtype))


def _attention(q, kv, kv_prev, bias, layer, nb, T):
    b, l, _ = q.shape
    mask_first = kv_prev is None
    if mask_first:
        prev_arr = kv
        prev_spec = pl.BlockSpec((nb, WINDOW, 2 * KV_WIDTH),
                                 lambda bi, i: (bi, jnp.maximum(i * (T // WINDOW) - 1, 0), 0))
    else:
        prev_arr = kv_prev
        prev_spec = pl.BlockSpec((None, nb, WINDOW, 2 * KV_WIDTH), lambda bi, i: (layer, bi, 0, 0))
    return pl.pallas_call(
        functools.partial(_attn_kernel, nb=nb, T=T, mask_first=mask_first),
        grid=(b // nb, l // T),
        in_specs=[pl.BlockSpec((nb, T, Q_WIDTH), lambda bi, i: (bi, i, 0)),
                  pl.BlockSpec((nb, T, 2 * KV_WIDTH), lambda bi, i: (bi, i, 0)),
                  prev_spec,
                  _layer(bias, layer)],
        out_specs=pl.BlockSpec((nb, T, Q_WIDTH), lambda bi, i: (bi, i, 0)),
        out_shape=jax.ShapeDtypeStruct((b, l, Q_WIDTH), BF16),
        scratch_shapes=[pltpu.VMEM((WINDOW + T, 2 * KV_WIDTH), F32)],
        compiler_params=pltpu.CompilerParams(dimension_semantics=("parallel", "parallel"),
                                             vmem_limit_bytes=VMEM_LIMIT),
        name="swa_attention",
    )(q, kv, prev_arr, bias)


def _load_state(st, h0_ref):
    for hp in range(HEAD_PAIRS):
        st[hp] = h0_ref[0, 2 * hp:2 * hp + 2].reshape(2 * SSM_HEAD_DIM, D_STATE).T


def _store_state(hout_ref, st):
    for hp in range(HEAD_PAIRS):
        hout_ref[0, 2 * hp:2 * hp + 2] = st[hp].T.reshape(2, SSM_HEAD_DIM, D_STATE)


def _ssd_kernel(z_ref, xbc_ref, dt_ref, hist_ref, h0_ref, *refs, T, has_acc):
    prm, refs = refs[:7], refs[7 + has_acc:]
    (y_ref, hout_ref, cbuf), scr = refs[:3], refs[3:]
    i = pl.program_id(1)

    @pl.when(i == 0)
    def _init():
        cbuf[0:8, :] = hist_ref[0]
        _load_state(scr[-1], h0_ref)

    cbuf[8:8 + T, :] = xbc_ref[0]
    _ssd_tile(z_ref.at[0], dt_ref.at[0], cbuf, y_ref.at[0], prm, scr, T)
    cbuf[0:8, :] = cbuf[T:T + 8, :]

    @pl.when(i == pl.num_programs(1) - 1)
    def _fin():
        _store_state(hout_ref, scr[-1])


def _ssd_tile(z_ref, dt_ref, cbuf, y_ref, prm, scr, T, side_work=()):
    for work in side_work:
        work()
    cw_ref, cb_ref, dtb_ref, alog_ref, dsk_ref, nw_ref, e64_ref = prm
    ubuf, ybuf, dtx_s, acs_s, st = scr
    row = lax.broadcasted_iota(jnp.int32, (CHUNK, LANES), 0)
    pos = lax.broadcasted_iota(jnp.int32, (CHUNK, LANES), 1) & (CHUNK - 1)
    diag = pos == row
    causal = pos <= row
    lo = lax.broadcasted_iota(jnp.int32, (1, LANES), 1) < SSM_HEAD_DIM
    tri = (lax.broadcasted_iota(jnp.int32, (CHUNK, CHUNK), 1)
           <= lax.broadcasted_iota(jnp.int32, (CHUNK, CHUNK), 0)).astype(F32).astype(BF16)
    a_neg = -jnp.exp(alog_ref[...])
    b_off = SSM_INNER
    c_off = SSM_INNER + SSM_GROUPS * D_STATE

    for j in range(T // CHUNK):
        r0 = 8 + j * CHUNK
        rows = slice(j * CHUNK, (j + 1) * CHUNK)
        for c0 in range(0, CONV_DIM, 512):
            cols = slice(c0, c0 + 512)
            xw = cbuf[r0 - 8:r0 + CHUNK, cols]
            xw1 = pltpu.roll(xw, 1, 0)
            near = xw * cw_ref[3:4, cols] + xw1 * cw_ref[2:3, cols]
            far = pltpu.roll(xw * cw_ref[1:2, cols] + xw1 * cw_ref[0:1, cols], 2, 0)
            acc = (near + far)[8:] + cb_ref[:, cols]
            ubuf[:, cols] = acc * _sigmoid(acc)

        dt_in = dt_ref[rows, :] + dtb_ref[...]
        dtv = jnp.maximum(dt_in, 0.0) + jnp.log1p(jnp.exp(-jnp.abs(dt_in)))
        cum3 = jnp.dot(tri, jnp.concatenate(_split3(dtv * a_neg), axis=1), preferred_element_type=F32)
        cum = cum3[:, 2 * LANES:] + cum3[:, LANES:2 * LANES] + cum3[:, :LANES]
        big = jnp.dot(jnp.concatenate(_split3(dtv) + _split3(cum), axis=0), e64_ref[...],
                      preferred_element_type=F32)
        dtx_s[...] = big[128:192] + big[64:128] + big[0:64]
        acs_s[...] = big[320:384] + big[256:320] + big[192:256]

        bb, bt, cb, cbd = [], [], [], []
        for g in range(SSM_GROUPS):
            bg = ubuf[:, b_off + g * D_STATE:b_off + (g + 1) * D_STATE]
            bb.append(bg.astype(BF16))
            bt.append(bg.T.astype(BF16))
            cb.append(ubuf[:, c_off + g * D_STATE:c_off + (g + 1) * D_STATE].astype(BF16))
            cbd.append(lax.dot_general(cb[g], jnp.concatenate([bb[g], bb[g]], axis=0), NT_DIMS,
                                       preferred_element_type=F32))

        for hp in range(HEAD_PAIRS):
            sl = slice(hp * LANES, (hp + 1) * LANES)
            g0, g1 = (2 * hp) // 3, (2 * hp + 1) // 3
            acs = acs_s[:, sl]
            acs_key = jnp.sum(jnp.where(diag, acs, 0.0), axis=0, keepdims=True)
            lmat = jnp.exp(jnp.where(causal, acs - acs_key, NEG))
            eacs = jnp.exp(acs)
            acs_last = acs[CHUNK - 1:CHUNK, :]
            xp = ubuf[:, sl]
            xdt = xp * dtx_s[:, sl]
            xlo = jnp.where(lo, xdt, 0.0)
            xhi = jnp.where(lo, 0.0, xdt)

            cbp = cbd[g0] if g0 == g1 else jnp.where(lo, cbd[g0], cbd[g1])
            xbd = jnp.concatenate([xlo, xhi], axis=0).astype(BF16)
            yd = jnp.dot((cbp * lmat).astype(BF16), xbd, preferred_element_type=F32)

            ps = st[hp]
            psb = ps.astype(BF16)
            yo = jnp.dot(cb[g0], psb, preferred_element_type=F32)
            if g0 != g1:
                yo = jnp.where(lo, yo, jnp.dot(cb[g1], psb, preferred_element_type=F32))
            ybuf[:, sl] = yd + yo * eacs + dsk_ref[:, sl] * xp

            wx = jnp.exp(acs_last - acs)
            if g0 == g1:
                new = jnp.dot(bt[g0], (xdt * wx).astype(BF16), preferred_element_type=F32)
            else:
                new = (jnp.dot(bt[g0], (xlo * wx).astype(BF16), preferred_element_type=F32)
                       + jnp.dot(bt[g1], (xhi * wx).astype(BF16), preferred_element_type=F32))
            st[hp] = ps * eacs[CHUNK - 1:CHUNK, :] + new

        zc = z_ref[rows, :]
        yg = ybuf[...] * (zc * _sigmoid(zc))
        y2 = yg * yg
        rs = []
        for g in range(SSM_GROUPS):
            g_lo, g_hi = g * SSM_GROUP_WIDTH, (g + 1) * SSM_GROUP_WIDTH
            w0 = (g_lo // LANES) * LANES
            w1 = -(-g_hi // LANES) * LANES
            lane = lax.broadcasted_iota(jnp.int32, (1, w1 - w0), 1) + w0
            ss = jnp.sum(jnp.where(lane < g_lo, 0.0, jnp.where(lane < g_hi, y2[:, w0:w1], 0.0)),
                         axis=-1, keepdims=True)
            rs.append(lax.rsqrt(ss / SSM_GROUP_WIDTH + EPS))
        lane = lax.broadcasted_iota(jnp.int32, (1, LANES), 1)
        for v in range(SSM_INNER // LANES):
            ga, gb = (v * LANES) // SSM_GROUP_WIDTH, (v * LANES + LANES - 1) // SSM_GROUP_WIDTH
            sc = rs[ga] if ga == gb else jnp.where(lane < gb * SSM_GROUP_WIDTH - v * LANES, rs[ga], rs[gb])
            vs = slice(v * LANES, (v + 1) * LANES)
            y_ref[rows, vs] = (yg[:, vs] * sc * nw_ref[:, vs]).astype(y_ref.dtype)


SSM_PARAMS = ("conv_w", "conv_b", "dt_bias", "a_log", "d_skip", "ssm_norm")


def _head_expand():
    return jnp.asarray(np.arange(LANES)[:, None] == (np.arange(SSM_INNER)[None, :] // SSM_HEAD_DIM), dtype=BF16)


def _ssd_scratch():
    return [pltpu.VMEM((CHUNK, CONV_DIM), F32), pltpu.VMEM((CHUNK, SSM_INNER), F32),
            pltpu.VMEM((CHUNK, SSM_INNER), F32), pltpu.VMEM((CHUNK, SSM_INNER), F32),
            pltpu.VMEM((HEAD_PAIRS, D_STATE, LANES), F32)]


def _ssd(z, xbc, dt, hist, h0, p, layer, T, h_acc):
    b, l, _ = z.shape
    tok = lambda w: pl.BlockSpec((1, T, w), lambda bi, i: (bi, i, 0))
    state = pl.BlockSpec((None, 1, SSM_HEADS, SSM_HEAD_DIM, D_STATE), lambda bi, i: (layer, bi, 0, 0, 0))
    acc = [] if h_acc is None else [h_acc]
    n_in = 5 + len(SSM_PARAMS) + 1
    return pl.pallas_call(
        functools.partial(_ssd_kernel, T=T, has_acc=len(acc)),
        grid=(b, l // T),
        in_specs=[tok(SSM_INNER), tok(CONV_DIM), tok(LANES),
                  pl.BlockSpec((None, 1, 8, CONV_DIM), lambda bi, i: (layer, bi, 0, 0)), state,
                  *[_layer(p[k], layer) for k in SSM_PARAMS],
                  _resident((LANES, SSM_INNER))] + [pl.BlockSpec(memory_space=pl.ANY) for _ in acc],
        out_specs=[tok(SSM_INNER), state],
        out_shape=[jax.ShapeDtypeStruct((b, l, SSM_INNER), BF16), jax.ShapeDtypeStruct(h0.shape, F32)],
        input_output_aliases={n_in: 1} if acc else {},
        scratch_shapes=[pltpu.VMEM((8 + T, CONV_DIM), F32)] + _ssd_scratch(),
        compiler_params=pltpu.CompilerParams(dimension_semantics=("parallel", "arbitrary"),
                                             vmem_limit_bytes=VMEM_LIMIT),
        name="conv_ssd",
    )(z, xbc, dt, hist, h0, *[p[k] for k in SSM_PARAMS], _head_expand(), *acc)


def _ssm_front_kernel(x_ref, nw_ref, wmain_ref, wside_ref, *refs, T, nt, has_acc):
    prm, refs = refs[:7], refs[7 + has_acc:]
    (y_ref, cq_ref, hout_ref, tail_ref), (zbuf, dtbuf, cbuf), scr = refs[:4], refs[4:7], refs[7:]
    s = pl.program_id(0)
    slot = lax.rem(s, 2)
    prev = 1 - slot
    i = lax.rem(s + nt - 1, nt)

    @pl.when(s == 0)
    def _first():
        zbuf[1] = jnp.zeros((T, SSM_INNER), F32)
        dtbuf[1] = jnp.zeros((T, LANES), F32)
        cbuf[1] = jnp.zeros((8 + T, CONV_DIM), F32)

    @pl.when((s == 0) | (i == 0))
    def _init():
        cbuf[prev, 0:8, :] = jnp.zeros((8, CONV_DIM), F32)
        for hp in range(HEAD_PAIRS):
            scr[-1][hp] = jnp.zeros((D_STATE, LANES), F32)

    h = _rms(x_ref[...], nw_ref[...]).astype(BF16)

    def project(w_ref, c0, width, store):
        return lambda: store(jnp.dot(h, w_ref[:, c0:c0 + width].astype(BF16), preferred_element_type=F32))

    def z_store(c0):
        def store(v):
            zbuf[slot, :, c0:c0 + 512] = v
        return store

    def xbc_store(c0):
        def store(v):
            cbuf[slot, 8:8 + T, c0:c0 + 512] = v
        return store

    def cq_store(v):
        cq_ref[...] = v.astype(cq_ref.dtype)

    def dt_store(v):
        dtbuf[slot] = v

    projections = ([project(wmain_ref, c0, 512, z_store(c0)) for c0 in range(0, SSM_INNER, 512)]
                   + [project(wmain_ref, SSM_INNER + c0, 512, xbc_store(c0)) for c0 in range(0, CONV_DIM, 512)]
                   + [project(wside_ref, 0, MEM_WIDTH, cq_store), project(wside_ref, MEM_WIDTH, LANES, dt_store)])
    _ssd_tile(zbuf.at[prev], dtbuf.at[prev], cbuf.at[prev], y_ref, prm, scr, T, projections)
    cbuf[slot, 0:8, :] = cbuf[prev, T:T + 8, :]

    @pl.when((s > 0) & (i == nt - 1))
    def _fin():
        _store_state(hout_ref, scr[-1])
        tail_ref[0] = cbuf[prev, T:T + 8, :]


def _ssm_front(x, p, i_layer, layer, w_main, b, T, h_acc):
    m = x.shape[0]
    nt = m // (b * T)
    n = b * nt
    n_ssm = p["conv_w"].shape[0]
    seq = lambda s: jnp.maximum(s - 1, 0) // nt
    state = pl.BlockSpec((None, 1, SSM_HEADS, SSM_HEAD_DIM, D_STATE), lambda s: (layer, seq(s), 0, 0, 0))
    acc = [] if h_acc is None else [h_acc]
    n_in = 4 + len(SSM_PARAMS) + 1
    return pl.pallas_call(
        functools.partial(_ssm_front_kernel, T=T, nt=nt, has_acc=len(acc)),
        grid=(n + 1,),
        in_specs=[pl.BlockSpec((T, D_MODEL), lambda s: (jnp.minimum(s, n - 1), 0)),
                  _layer(p["norm_mix_pre"], i_layer), _resident(w_main.shape), _layer(p["w_in_side"], layer),
                  *[_layer(p[k], layer) for k in SSM_PARAMS],
                  _resident((LANES, SSM_INNER))] + [pl.BlockSpec(memory_space=pl.ANY) for _ in acc],
        out_specs=[pl.BlockSpec((T, SSM_INNER), lambda s: (jnp.maximum(s - 1, 0), 0)),
                   pl.BlockSpec((T, MEM_WIDTH), lambda s: (jnp.minimum(s, n - 1), 0)),
                   state,
                   pl.BlockSpec((1, 8, CONV_DIM), lambda s: (seq(s), 0, 0))],
        out_shape=[jax.ShapeDtypeStruct((m, SSM_INNER), BF16), jax.ShapeDtypeStruct((m, MEM_WIDTH), BF16),
                   jax.ShapeDtypeStruct((n_ssm, b, SSM_HEADS, SSM_HEAD_DIM, D_STATE), F32),
                   jax.ShapeDtypeStruct((b, 8, CONV_DIM), F32)],
        input_output_aliases={n_in: 2} if acc else {},
        scratch_shapes=[pltpu.VMEM((2, T, SSM_INNER), F32), pltpu.VMEM((2, T, LANES), F32),
                        pltpu.VMEM((2, 8 + T, CONV_DIM), F32)] + _ssd_scratch(),
        compiler_params=pltpu.CompilerParams(dimension_semantics=("arbitrary",), vmem_limit_bytes=VMEM_LIMIT),
        name="ssm_front",
    )(x, p["norm_mix_pre"], w_main, p["w_in_side"], *[p[k] for k in SSM_PARAMS], _head_expand(), *acc)


def _tail_kernel(a_ref, cq_ref, mk_ref, mv_ref, x_ref, wout_ref, npost_ref, nfpre_ref, wup_ref, wdn_ref, nfpost_ref,
                 *refs, nb, T):
    n_cast = len(refs) // 2
    o_ref = refs[n_cast]
    for src, dst in zip(refs[:n_cast], refs[n_cast + 1:]):
        dst[...] = src[...].astype(dst.dtype)
    m = nb * T
    cs = []
    for n in range(nb):
        outs = []
        for h in range(MEM_HEADS):
            hs = slice(h * MEM_HEAD_DIM, (h + 1) * MEM_HEAD_DIM)
            head_rows = pl.ds(h, N_MEM, stride=MEM_HEADS)
            mk = mk_ref[n, head_rows, :].astype(BF16)
            mv = mv_ref[n, head_rows, :].astype(BF16)
            s = lax.dot_general(cq_ref[n, :, hs], mk, NT_DIMS, preferred_element_type=F32)
            s = s * (1.0 / math.sqrt(MEM_HEAD_DIM))
            p = jnp.exp(s - jnp.max(s, axis=-1, keepdims=True))
            den = jnp.sum(p, axis=-1, keepdims=True)
            o = jnp.dot(p.astype(BF16), mv, preferred_element_type=F32) * (1.0 / den)
            outs.append(o.astype(BF16))
        cs.append(jnp.concatenate(outs, axis=1))
    c = cs[0] if nb == 1 else jnp.concatenate(cs, axis=0)
    a = a_ref[...].reshape(m, Q_WIDTH)
    mix = (jnp.dot(a, wout_ref[0:Q_WIDTH, :], preferred_element_type=F32)
           + jnp.dot(c, wout_ref[Q_WIDTH:, :], preferred_element_type=F32))
    x1 = x_ref[...].reshape(m, D_MODEL) + _rms(mix, npost_ref[...])
    h2 = _rms(x1, nfpre_ref[...]).astype(BF16)
    acc = None
    for f0 in range(0, D_FF, 1024):
        u = jnp.maximum(jnp.dot(h2, wup_ref[:, f0:f0 + 1024], preferred_element_type=F32), 0.0)
        d = jnp.dot((u * u).astype(BF16), wdn_ref[f0:f0 + 1024, :], preferred_element_type=F32)
        acc = d if acc is None else acc + d
    o_ref[...] = (x1 + _rms(acc, nfpost_ref[...])).reshape(nb, T, D_MODEL)


def _tail(a, cq, mk, mv, x, p, layer, w_out, w_up, w_down, nb, T, cast=()):
    b, l, _ = x.shape
    steps = (b // nb) * (l // T)
    tok = lambda w: pl.BlockSpec((nb, T, w), lambda bi, i: (bi, i, 0))
    mem = pl.BlockSpec((None, nb, N_MEM * MEM_HEADS, MEM_HEAD_DIM), lambda bi, i: (layer, bi, 0, 0))
    norm = lambda k: _layer(p[k], layer)
    slab = lambda bi, i: bi * (l // T) + i
    cast_in = [pl.BlockSpec((None, w.shape[1] // steps, w.shape[2]), lambda bi, i, lw=lw: (lw, slab(bi, i), 0))
               for w, lw in cast]
    cast_out = [pl.BlockSpec((w.shape[1] // steps, w.shape[2]), lambda bi, i: (slab(bi, i), 0)) for w, _ in cast]
    return pl.pallas_call(
        functools.partial(_tail_kernel, nb=nb, T=T),
        grid=(b // nb, l // T),
        in_specs=[tok(Q_WIDTH), tok(MEM_WIDTH), mem, mem, tok(D_MODEL),
                  _resident(w_out.shape), norm("norm_mix_post"), norm("norm_ffn_pre"), _resident(w_up.shape),
                  _resident(w_down.shape), norm("norm_ffn_post")] + cast_in,
        out_specs=[tok(D_MODEL)] + cast_out,
        out_shape=[jax.ShapeDtypeStruct((b, l, D_MODEL), F32)]
        + [jax.ShapeDtypeStruct(w.shape[1:], BF16) for w, _ in cast],
        compiler_params=pltpu.CompilerParams(dimension_semantics=("parallel", "parallel"),
                                             vmem_limit_bytes=VMEM_LIMIT),
        name="xattn_outproj_mlp",
    )(a, cq, mk, mv, x, w_out, p["norm_mix_post"], p["norm_ffn_pre"], w_up, w_down, p["norm_ffn_post"],
      *[w for w, _ in cast])


ATTN_SEGS = ((0, Q_WIDTH), (Q_WIDTH, 2 * KV_WIDTH), (Q_WIDTH + 2 * KV_WIDTH, MEM_WIDTH))
SSM_MAIN_SEGS = ((0, SSM_INNER), (SSM_INNER, CONV_DIM))
SSM_SIDE_SEGS = ((0, MEM_WIDTH), (MEM_WIDTH, LANES))


def _prep_weights(norm_mix_pre, norm_mix_post, norm_ffn_pre, norm_ffn_post, attn_sinks, w_in_ssm, conv_w,
                  conv_b, dt_bias, a_log, d_skip, ssm_norm):
    dt_lo = SSM_INNER + CONV_DIM
    dt_hi = dt_lo + SSM_HEADS
    pad = jnp.zeros((w_in_ssm.shape[0], D_MODEL, LANES - SSM_HEADS), F32)
    w_side = jnp.concatenate([w_in_ssm[..., dt_hi:], w_in_ssm[..., dt_lo:dt_hi], pad], axis=-1)
    vec = lambda v: v.astype(F32).reshape(v.shape[0], 1, -1)
    lane_pad = lambda v: vec(jnp.pad(v, ((0, 0), (0, LANES - SSM_HEADS))))
    return dict(
        norm_mix_pre=vec(norm_mix_pre), norm_mix_post=vec(norm_mix_post), norm_ffn_pre=vec(norm_ffn_pre),
        norm_ffn_post=vec(norm_ffn_post), w_in_side=w_side,
        attn_bias=jnp.stack([_attn_bias_table(attn_sinks[j]) for j in range(attn_sinks.shape[0])]),
        conv_w=conv_w.astype(F32), conv_b=vec(conv_b), dt_bias=lane_pad(dt_bias), a_log=lane_pad(a_log),
        d_skip=vec(jnp.repeat(d_skip, SSM_HEAD_DIM, axis=-1)), ssm_norm=vec(ssm_norm))


BIG_WEIGHTS = ("w_in", "w_out", "w_up", "w_down")


def _trunk(x, mem_k, mem_v, swa_prev, ssm_h0, conv_hist, p, big, raw, *, tm, attn_tile, ssd_tile, tail_tile):
    b, l, _ = x.shape
    m = b * l
    new_kv, new_conv, states = [], [], None
    for i in range(DEPTH):
        j = i // 2
        xf = x.reshape(m, D_MODEL)
        if i % 2 == 0:
            q, kv, cq = _norm_matmul(xf, p["norm_mix_pre"], i, [(big["w_in"][i], ATTN_SEGS, (BF16, F32, BF16))], j, tm)
            kv = kv.reshape(b, l, 2 * KV_WIDTH)
            a = _attention(q.reshape(b, l, Q_WIDTH), kv, swa_prev, p["attn_bias"], j, *attn_tile)
            new_kv.append(kv)
        elif ssm_h0 is None:
            a, cq, states, conv_tail = _ssm_front(xf, p, i, j, big["w_in"][i], b, ssd_tile, states)
            a = a.reshape(b, l, SSM_INNER)
            new_conv.append(conv_tail[:, 8 - (CONV_W - 1):])
        else:
            z, xbc, cq, dt = _norm_matmul(xf, p["norm_mix_pre"], i,
                                          [(big["w_in"][i], SSM_MAIN_SEGS, (F32, F32)),
                                           (p["w_in_side"], SSM_SIDE_SEGS, (BF16, F32))], j, tm)
            xbc = xbc.reshape(b, l, CONV_DIM)
            a, states = _ssd(z.reshape(b, l, SSM_INNER), xbc, dt.reshape(b, l, LANES), conv_hist, ssm_h0, p, j,
                             ssd_tile, states)
            new_conv.append(xbc[:, l - (CONV_W - 1):])
        cast = [raw[k][i + 1] for k in BIG_WEIGHTS] if raw is not None and i + 1 < DEPTH else []
        x, *copies = _tail(a, cq.reshape(b, l, MEM_WIDTH), mem_k, mem_v, x, p, i, big["w_out"][i], big["w_up"][i],
                           big["w_down"][i], *tail_tile, cast)
        for k, w in zip(BIG_WEIGHTS, copies):
            big[k][i + 1] = w
    return x, new_kv, states, jnp.stack(new_conv)


def kernel(x_prompt, x_sample, mem_prompt, cache_swa_k, cache_swa_v, state_ssm, state_conv, cache_mem_k, cache_mem_v,
           norm_mix_pre, norm_mix_post, norm_ffn_pre, norm_ffn_post, w_in_attn, attn_sinks, w_in_ssm, conv_w, conv_b,
           dt_bias, a_log, d_skip, ssm_norm, mem_norm, w_mem_kv, w_out, w_up, w_down):
    p = _prep_weights(norm_mix_pre, norm_mix_post, norm_ffn_pre, norm_ffn_post, attn_sinks, w_in_ssm, conv_w, conv_b,
                      dt_bias, a_log, d_skip, ssm_norm)
    bp, lp, _ = x_prompt.shape
    bs, ls, _ = x_sample.shape
    n_attn = w_in_attn.shape[0]
    raw = dict(w_in=[(w_in_ssm if i % 2 else w_in_attn, i // 2) for i in range(DEPTH)],
               w_out=[(w_out, i) for i in range(DEPTH)], w_up=[(w_up, i) for i in range(DEPTH)],
               w_down=[(w_down, i) for i in range(DEPTH)])
    big = {k: [raw[k][0][0][0].astype(BF16)] + [None] * (DEPTH - 1) for k in BIG_WEIGHTS}

    mem_rows = N_MEM * MEM_HEADS
    prompt_mem_k, prompt_mem_v = _mem_kv(mem_prompt.reshape(bp * N_MEM, D_MODEL),
                                         mem_norm.astype(F32).reshape(DEPTH, 1, D_MODEL), w_mem_kv.astype(BF16), 512)

    y_prompt, kv_p, prompt_ssm, prompt_conv = _trunk(
        x_prompt, prompt_mem_k.reshape(DEPTH, bp, mem_rows, MEM_HEAD_DIM),
        prompt_mem_v.reshape(DEPTH, bp, mem_rows, MEM_HEAD_DIM), None, None, None, p, big, raw,
        tm=512, attn_tile=(1, 256), ssd_tile=256, tail_tile=(1, 512))

    keep = cache_swa_k.shape[2]
    swa_prev = jnp.concatenate([cache_swa_k.reshape(n_attn, bs, keep, KV_WIDTH),
                                cache_swa_v.reshape(n_attn, bs, keep, KV_WIDTH)], axis=-1)
    hist = jnp.pad(state_conv, ((0, 0), (0, 0), (8 - (CONV_W - 1), 0), (0, 0)))
    y_sample, kv_s, sample_ssm, sample_conv = _trunk(
        x_sample, cache_mem_k.reshape(DEPTH, bs, mem_rows, MEM_HEAD_DIM),
        cache_mem_v.reshape(DEPTH, bs, mem_rows, MEM_HEAD_DIM), swa_prev, state_ssm, hist, p, big, None,
        tm=512, attn_tile=(4, ls), ssd_tile=ls, tail_tile=(8, ls))

    def split_kv(kvs, rows):
        kv = jnp.stack([kv[:, -rows:] for kv in kvs])
        shape = kv.shape[:3] + (N_KV_HEADS, HEAD_DIM)
        return kv[..., :KV_WIDTH].reshape(shape), kv[..., KV_WIDTH:].reshape(shape)

    prompt_swa_k, prompt_swa_v = split_kv(kv_p, min(WINDOW, lp))
    sample_swa_k, sample_swa_v = split_kv(kv_s, ls)
    mem_shape = (DEPTH, bp, N_MEM, MEM_HEADS, MEM_HEAD_DIM)
    return (y_prompt, y_sample, prompt_swa_k, prompt_swa_v, prompt_ssm, prompt_conv,
            prompt_mem_k.reshape(mem_shape), prompt_mem_v.reshape(mem_shape),
            sample_swa_k, sample_swa_v, sample_ssm, sample_conv)
```

```python
import functools
import math

import numpy as np
import jax
import jax.numpy as jnp
from jax import lax
from jax.experimental import pallas as pl
from jax.experimental.pallas import tpu as pltpu

F32 = jnp.float32
BF16 = jnp.bfloat16

D_MODEL = 1024
DEPTH = 4
CHUNK = 64
N_HEADS = 24
N_KV_HEADS = 3
HEAD_DIM = 64
WINDOW = 128
Q_WIDTH = N_HEADS * HEAD_DIM
KV_WIDTH = N_KV_HEADS * HEAD_DIM
N_MEM = 256
MEM_HEADS = 4
MEM_HEAD_DIM = 128
MEM_WIDTH = MEM_HEADS * MEM_HEAD_DIM
SSM_INNER = 1536
SSM_HEAD_DIM = 64
SSM_HEADS = SSM_INNER // SSM_HEAD_DIM
SSM_GROUPS = 8
SSM_GROUP_WIDTH = SSM_INNER // SSM_GROUPS
D_STATE = 128
CONV_W = 4
CONV_DIM = SSM_INNER + 2 * SSM_GROUPS * D_STATE
D_FF = 4 * D_MODEL
EPS = 1e-6

LANES = 128
HEAD_PAIRS = SSM_HEADS // 2
NEG = -1e30
VMEM_LIMIT = 56 * 1024 * 1024

NT_DIMS = (((1,), (1,)), ((), ()))


def _rms(x, w):
    return x * lax.rsqrt(jnp.mean(x * x, axis=-1, keepdims=True) + EPS) * w


def _sigmoid(x):
    return 1.0 / (1.0 + jnp.exp(-x))


def _split3(x):
    a = x.astype(BF16)
    r = x - a.astype(F32)
    b = r.astype(BF16)
    c = (r - b.astype(F32)).astype(BF16)
    return a, b, c


def _resident(shape):
    return pl.BlockSpec(shape, lambda *_: (0,) * len(shape), pipeline_mode=pl.Buffered(1))


def _layer(stacked, layer):
    shape = stacked.shape[1:]
    return pl.BlockSpec((None,) + shape, lambda *_: (layer,) + (0,) * len(shape), pipeline_mode=pl.Buffered(1))


def _norm_matmul_kernel(x_ref, nw_ref, *refs, plan):
    w_refs, o_refs = refs[:len(plan)], list(refs[len(plan):])
    h = _rms(x_ref[...], nw_ref[...]).astype(BF16)
    for w_ref, segs in zip(w_refs, plan):
        for start, width in segs:
            o_ref = o_refs.pop(0)
            for c0 in range(0, width, 512):
                cw = min(512, width - c0)
                o_ref[:, c0:c0 + cw] = jnp.dot(
                    h, w_ref[:, start + c0:start + c0 + cw].astype(BF16), preferred_element_type=F32).astype(o_ref.dtype)


def _norm_matmul(x, nw, nw_layer, weights, w_layer, tm):
    m, k = x.shape
    plan = tuple(segs for _, segs, _ in weights)
    widths = [wd for _, segs, _ in weights for _, wd in segs]
    dtypes = [dt for _, _, dts in weights for dt in dts]
    return pl.pallas_call(
        functools.partial(_norm_matmul_kernel, plan=plan),
        grid=(m // tm,),
        in_specs=([pl.BlockSpec((tm, k), lambda i: (i, 0)), _layer(nw, nw_layer)]
                  + [_layer(w, w_layer) if w.ndim == 3 else _resident(w.shape) for w, _, _ in weights]),
        out_specs=[pl.BlockSpec((tm, wd), lambda i: (i, 0)) for wd in widths],
        out_shape=[jax.ShapeDtypeStruct((m, wd), dt) for wd, dt in zip(widths, dtypes)],
        compiler_params=pltpu.CompilerParams(dimension_semantics=("parallel",), vmem_limit_bytes=VMEM_LIMIT),
        name="norm_matmul",
    )(x, nw, *[w for w, _, _ in weights])


def _mem_kv_kernel(x_ref, nw_ref, w_ref, k_ref, v_ref):
    tm = x_ref.shape[0]
    h = _rms(x_ref[...], nw_ref[...]).astype(BF16)
    for o_ref, c0 in ((k_ref, 0), (v_ref, MEM_WIDTH)):
        r = jnp.dot(h, w_ref[:, c0:c0 + MEM_WIDTH], preferred_element_type=F32)
        for hd in range(MEM_HEADS):
            o_ref[pl.ds(hd, tm, stride=MEM_HEADS), :] = r[:, hd * MEM_HEAD_DIM:(hd + 1) * MEM_HEAD_DIM]


def _mem_kv(mem, nw, w, tm):
    m, k = mem.shape
    out = jax.ShapeDtypeStruct((DEPTH, m * MEM_HEADS, MEM_HEAD_DIM), F32)
    out_spec = pl.BlockSpec((None, tm * MEM_HEADS, MEM_HEAD_DIM), lambda l, i: (l, i, 0))
    return pl.pallas_call(
        _mem_kv_kernel,
        grid=(DEPTH, m // tm),
        in_specs=[pl.BlockSpec((tm, k), lambda l, i: (i, 0)),
                  pl.BlockSpec((None, 1, k), lambda l, i: (l, 0, 0)),
                  pl.BlockSpec((None, k, 2 * MEM_WIDTH), lambda l, i: (l, 0, 0))],
        out_specs=[out_spec, out_spec],
        out_shape=[out, out],
        compiler_params=pltpu.CompilerParams(dimension_semantics=("parallel", "parallel"),
                                             vmem_limit_bytes=VMEM_LIMIT),
        name="mem_kv",
    )(mem, nw, w)


def _attn_bias_table(sinks):
    slopes = jnp.asarray(2.0 ** (-8.0 * np.arange(1, N_HEADS + 1) / N_HEADS), dtype=F32).reshape(N_KV_HEADS, 4, 1, 2, 1)
    span = WINDOW + CHUNK
    i = np.arange(CHUNK)[:, None]
    s = np.arange(2 * LANES)[None, :]
    dist = jnp.asarray(np.abs(WINDOW + i - s), dtype=F32)[None, None, :, None, :]
    slot = jnp.asarray(s)[None, None, :, None, :]
    bias = -slopes * dist
    bias = jnp.where(slot < span, bias, NEG)
    bias = jnp.where(slot == span, sinks.astype(F32).reshape(N_KV_HEADS, 4, 1, 2, 1), bias)
    return bias.reshape(N_KV_HEADS, 4 * CHUNK, 4 * LANES)


def _attn_kernel(q_ref, kvc_ref, kvp_ref, bias_ref, o_ref, kvbuf, *, nb, T, mask_first):
    first = pl.program_id(1) == 0
    lo = lax.broadcasted_iota(jnp.int32, (1, LANES), 1) < HEAD_DIM
    slot = lax.broadcasted_iota(jnp.int32, (1, 4 * LANES), 1) & (2 * LANES - 1)
    zpad = jnp.zeros((CHUNK, LANES), BF16)
    ones_halves = jnp.where(
        (lax.broadcasted_iota(jnp.int32, (4 * LANES, LANES), 0) < 2 * LANES)
        == (lax.broadcasted_iota(jnp.int32, (4 * LANES, LANES), 1) < HEAD_DIM), 1.0, 0.0).astype(BF16)
    span = WINDOW + CHUNK
    for n in range(nb):
        kvbuf[0:WINDOW, :] = kvp_ref[n]
        kvbuf[WINDOW:WINDOW + T, :] = kvc_ref[n]
        for j in range(T // CHUNK):
            win = kvbuf[j * CHUNK:j * CHUNK + span, :]
            blocks = [win[:, m * LANES:(m + 1) * LANES] for m in range(3)]
            rolled = [pltpu.roll(b, HEAD_DIM, 1) for b in blocks]

            def halves(idx):
                m, half = divmod(idx, 2)
                in_lo, in_hi = (blocks[m], rolled[m]) if half == 0 else (rolled[m], blocks[m])
                return jnp.where(lo, in_lo, 0.0).astype(BF16), jnp.where(lo, 0.0, in_hi).astype(BF16)

            for kvh in range(N_KV_HEADS):
                klo, khi = halves(kvh)
                vlo, vhi = halves(N_KV_HEADS + kvh)
                kbd = jnp.concatenate([klo, zpad, khi, zpad], axis=0)
                c0 = kvh * 4 * LANES
                qs = jnp.concatenate(
                    [q_ref[n, j * CHUNK:(j + 1) * CHUNK, c0 + p * LANES:c0 + (p + 1) * LANES] for p in range(4)],
                    axis=0)
                s = lax.dot_general(qs * (1.0 / math.sqrt(HEAD_DIM)), kbd, NT_DIMS, preferred_element_type=F32)
                s = s + bias_ref[kvh]
                if mask_first and j < WINDOW // CHUNK:
                    s = jnp.where(slot < jnp.where(first, WINDOW - j * CHUNK, 0), NEG, s)
                ps = []
                for e in range(2):
                    se = s[:, e * 2 * LANES:(e + 1) * 2 * LANES]
                    ps.append(jnp.exp(se - jnp.max(se, axis=-1, keepdims=True)).astype(BF16))
                od = jnp.dot(jnp.concatenate(ps, axis=1),
                             jnp.concatenate([jnp.concatenate([vlo, zpad, vhi, zpad], axis=0), ones_halves], axis=1),
                             preferred_element_type=F32)
                o = od[:, :LANES] / od[:, LANES:]
                for p in range(4):
                    o_ref[n, j * CHUNK:(j + 1) * CHUNK, c0 + p * LANES:c0 + (p + 1) * LANES] = (
                        o[p * CHUNK:(p + 1) * CHUNK].astype(o_ref.dtype))


def _attention(q, kv, kv_prev, bias, layer, nb, T):
    b, l, _ = q.shape
    mask_first = kv_prev is None
    if mask_first:
        prev_arr = kv
        prev_spec = pl.BlockSpec((nb, WINDOW, 2 * KV_WIDTH),
                                 lambda bi, i: (bi, jnp.maximum(i * (T // WINDOW) - 1, 0), 0))
    else:
        prev_arr = kv_prev
        prev_spec = pl.BlockSpec((None, nb, WINDOW, 2 * KV_WIDTH), lambda bi, i: (layer, bi, 0, 0))
    return pl.pallas_call(
        functools.partial(_attn_kernel, nb=nb, T=T, mask_first=mask_first),
        grid=(b // nb, l // T),
        in_specs=[pl.BlockSpec((nb, T, Q_WIDTH), lambda bi, i: (bi, i, 0)),
                  pl.BlockSpec((nb, T, 2 * KV_WIDTH), lambda bi, i: (bi, i, 0)),
                  prev_spec,
                  _layer(bias, layer)],
        out_specs=pl.BlockSpec((nb, T, Q_WIDTH), lambda bi, i: (bi, i, 0)),
        out_shape=jax.ShapeDtypeStruct((b, l, Q_WIDTH), BF16),
        scratch_shapes=[pltpu.VMEM((WINDOW + T, 2 * KV_WIDTH), F32)],
        compiler_params=pltpu.CompilerParams(dimension_semantics=("parallel", "parallel"),
                                             vmem_limit_bytes=VMEM_LIMIT),
        name="swa_attention",
    )(q, kv, prev_arr, bias)


def _load_state(st, h0_ref):
    for hp in range(HEAD_PAIRS):
        st[hp] = h0_ref[0, 2 * hp:2 * hp + 2].reshape(2 * SSM_HEAD_DIM, D_STATE).T


def _store_state(hout_ref, st):
    for hp in range(HEAD_PAIRS):
        hout_ref[0, 2 * hp:2 * hp + 2] = st[hp].T.reshape(2, SSM_HEAD_DIM, D_STATE)


def _ssd_kernel(z_ref, xbc_ref, dt_ref, hist_ref, h0_ref, *refs, T, has_acc):
    prm, refs = refs[:7], refs[7 + has_acc:]
    (y_ref, hout_ref, cbuf), scr = refs[:3], refs[3:]
    i = pl.program_id(1)

    @pl.when(i == 0)
    def _init():
        cbuf[0:8, :] = hist_ref[0]
        _load_state(scr[-1], h0_ref)

    cbuf[8:8 + T, :] = xbc_ref[0]
    _ssd_tile(z_ref.at[0], dt_ref.at[0], cbuf, y_ref.at[0], prm, scr, T)
    cbuf[0:8, :] = cbuf[T:T + 8, :]

    @pl.when(i == pl.num_programs(1) - 1)
    def _fin():
        _store_state(hout_ref, scr[-1])


def _ssd_tile(z_ref, dt_ref, cbuf, y_ref, prm, scr, T, side_work=()):
    for work in side_work:
        work()
    cw_ref, cb_ref, dtb_ref, alog_ref, dsk_ref, nw_ref, e64_ref = prm
    ubuf, ybuf, dtx_s, acs_s, st = scr
    row = lax.broadcasted_iota(jnp.int32, (CHUNK, LANES), 0)
    pos = lax.broadcasted_iota(jnp.int32, (CHUNK, LANES), 1) & (CHUNK - 1)
    diag = pos == row
    causal = pos <= row
    lo = lax.broadcasted_iota(jnp.int32, (1, LANES), 1) < SSM_HEAD_DIM
    tri = (lax.broadcasted_iota(jnp.int32, (CHUNK, CHUNK), 1)
           <= lax.broadcasted_iota(jnp.int32, (CHUNK, CHUNK), 0)).astype(F32).astype(BF16)
    a_neg = -jnp.exp(alog_ref[...])
    b_off = SSM_INNER
    c_off = SSM_INNER + SSM_GROUPS * D_STATE

    for j in range(T // CHUNK):
        r0 = 8 + j * CHUNK
        rows = slice(j * CHUNK, (j + 1) * CHUNK)
        for c0 in range(0, CONV_DIM, LANES):
            cols = slice(c0, c0 + LANES)
            xw = cbuf[r0 - 8:r0 + CHUNK, cols]
            xw1 = pltpu.roll(xw, 1, 0)
            near = xw * cw_ref[3:4, cols] + xw1 * cw_ref[2:3, cols]
            far = pltpu.roll(xw * cw_ref[1:2, cols] + xw1 * cw_ref[0:1, cols], 2, 0)
            acc = (near + far)[8:] + cb_ref[:, cols]
            ubuf[:, cols] = acc * _sigmoid(acc)

        dt_in = dt_ref[rows, :] + dtb_ref[...]
        dtv = jnp.maximum(dt_in, 0.0) + jnp.log1p(jnp.exp(-jnp.abs(dt_in)))
        cum3 = jnp.dot(tri, jnp.concatenate(_split3(dtv * a_neg), axis=1), preferred_element_type=F32)
        cum = cum3[:, 2 * LANES:] + cum3[:, LANES:2 * LANES] + cum3[:, :LANES]
        terms = jnp.concatenate([jnp.concatenate(_split3(dtv), axis=1), jnp.concatenate(_split3(cum), axis=1)], axis=0)
        both = jnp.dot(terms, e64_ref[...], preferred_element_type=F32)
        dtx_s[...] = both[:CHUNK]
        acs_s[...] = both[CHUNK:]

        bb, bt, cb, cbd = [], [], [], []
        for g in range(SSM_GROUPS):
            bg = ubuf[:, b_off + g * D_STATE:b_off + (g + 1) * D_STATE]
            bb.append(bg.astype(BF16))
            bt.append(bg.T.astype(BF16))
            cb.append(ubuf[:, c_off + g * D_STATE:c_off + (g + 1) * D_STATE].astype(BF16))
            cbd.append(lax.dot_general(cb[g], jnp.concatenate([bb[g], bb[g]], axis=0), NT_DIMS,
                                       preferred_element_type=F32))

        for hp in range(HEAD_PAIRS):
            sl = slice(hp * LANES, (hp + 1) * LANES)
            g0, g1 = (2 * hp) // 3, (2 * hp + 1) // 3
            acs = acs_s[:, sl]
            acs_key = jnp.sum(jnp.where(diag, acs, 0.0), axis=0, keepdims=True)
            lmat = jnp.exp(jnp.where(causal, acs - acs_key, NEG))
            eacs = jnp.exp(acs)
            acs_last = acs[CHUNK - 1:CHUNK, :]
            xp = ubuf[:, sl]
            xdt = xp * dtx_s[:, sl]
            xlo = jnp.where(lo, xdt, 0.0)
            xhi = jnp.where(lo, 0.0, xdt)

            cbp = cbd[g0] if g0 == g1 else jnp.where(lo, cbd[g0], cbd[g1])
            xbd = jnp.concatenate([xlo, xhi], axis=0).astype(BF16)
            yd = jnp.dot((cbp * lmat).astype(BF16), xbd, preferred_element_type=F32)

            ps = st[hp]
            psb = ps.astype(BF16)
            yo = jnp.dot(cb[g0], psb, preferred_element_type=F32)
            if g0 != g1:
                yo = jnp.where(lo, yo, jnp.dot(cb[g1], psb, preferred_element_type=F32))
            ybuf[:, sl] = yd + yo * eacs + dsk_ref[:, sl] * xp

            wx = jnp.exp(acs_last - acs)
            if g0 == g1:
                new = jnp.dot(bt[g0], (xdt * wx).astype(BF16), preferred_element_type=F32)
            else:
                new = (jnp.dot(bt[g0], (xlo * wx).astype(BF16), preferred_element_type=F32)
                       + jnp.dot(bt[g1], (xhi * wx).astype(BF16), preferred_element_type=F32))
            st[hp] = ps * eacs[CHUNK - 1:CHUNK, :] + new

        lane = lax.broadcasted_iota(jnp.int32, (1, LANES), 1)
        blocks = []
        for v in range(SSM_INNER // LANES):
            ga, gb = (v * LANES) // SSM_GROUP_WIDTH, (v * LANES + LANES - 1) // SSM_GROUP_WIDTH
            blocks.append((slice(v * LANES, (v + 1) * LANES), ga, gb, lane < gb * SSM_GROUP_WIDTH - v * LANES))
        ss = [None] * SSM_GROUPS
        for vs, ga, gb, in_ga in blocks:
            zv = z_ref[rows, vs]
            yv = ybuf[:, vs] * (zv * _sigmoid(zv))
            ybuf[:, vs] = yv
            y2 = yv * yv
            parts = [(ga, y2)] if ga == gb else [(ga, jnp.where(in_ga, y2, 0.0)), (gb, jnp.where(in_ga, 0.0, y2))]
            for g, part in parts:
                t = jnp.sum(part, axis=-1, keepdims=True)
                ss[g] = t if ss[g] is None else ss[g] + t
        rs = [lax.rsqrt(t / SSM_GROUP_WIDTH + EPS) for t in ss]
        for vs, ga, gb, in_ga in blocks:
            sc = rs[ga] if ga == gb else jnp.where(in_ga, rs[ga], rs[gb])
            y_ref[rows, vs] = (ybuf[:, vs] * sc * nw_ref[:, vs]).astype(y_ref.dtype)


SSM_PARAMS = ("conv_w", "conv_b", "dt_bias", "a_log", "d_skip", "ssm_norm")


def _head_expand():
    one = np.arange(LANES)[:, None] == (np.arange(SSM_INNER)[None, :] // SSM_HEAD_DIM)
    return jnp.asarray(np.tile(one, (3, 1)), dtype=BF16)


def _ssd_scratch():
    return [pltpu.VMEM((CHUNK, CONV_DIM), F32), pltpu.VMEM((CHUNK, SSM_INNER), F32),
            pltpu.VMEM((CHUNK, SSM_INNER), F32), pltpu.VMEM((CHUNK, SSM_INNER), F32),
            pltpu.VMEM((HEAD_PAIRS, D_STATE, LANES), F32)]


def _ssd(z, xbc, dt, hist, h0, p, layer, T, h_acc):
    b, l, _ = z.shape
    tok = lambda w: pl.BlockSpec((1, T, w), lambda bi, i: (bi, i, 0))
    state = pl.BlockSpec((None, 1, SSM_HEADS, SSM_HEAD_DIM, D_STATE), lambda bi, i: (layer, bi, 0, 0, 0))
    acc = [] if h_acc is None else [h_acc]
    n_in = 5 + len(SSM_PARAMS) + 1
    return pl.pallas_call(
        functools.partial(_ssd_kernel, T=T, has_acc=len(acc)),
        grid=(b, l // T),
        in_specs=[tok(SSM_INNER), tok(CONV_DIM), tok(LANES),
                  pl.BlockSpec((None, 1, 8, CONV_DIM), lambda bi, i: (layer, bi, 0, 0)), state,
                  *[_layer(p[k], layer) for k in SSM_PARAMS],
                  _resident((3 * LANES, SSM_INNER))] + [pl.BlockSpec(memory_space=pl.ANY) for _ in acc],
        out_specs=[tok(SSM_INNER), state],
        out_shape=[jax.ShapeDtypeStruct((b, l, SSM_INNER), BF16), jax.ShapeDtypeStruct(h0.shape, F32)],
        input_output_aliases={n_in: 1} if acc else {},
        scratch_shapes=[pltpu.VMEM((8 + T, CONV_DIM), F32)] + _ssd_scratch(),
        compiler_params=pltpu.CompilerParams(dimension_semantics=("parallel", "arbitrary"),
                                             vmem_limit_bytes=VMEM_LIMIT),
        name="conv_ssd",
    )(z, xbc, dt, hist, h0, *[p[k] for k in SSM_PARAMS], _head_expand(), *acc)


def _ssm_front_kernel(x_ref, nw_ref, wmain_ref, wside_ref, *refs, T, nt, has_acc):
    prm, refs = refs[:7], refs[7 + has_acc:]
    (y_ref, cq_ref, hout_ref, tail_ref), (zbuf, dtbuf, cbuf), scr = refs[:4], refs[4:7], refs[7:]
    s = pl.program_id(0)
    slot = lax.rem(s, 2)
    prev = 1 - slot
    i = lax.rem(s + nt - 1, nt)

    @pl.when(s == 0)
    def _first():
        zbuf[1] = jnp.zeros((T, SSM_INNER), F32)
        dtbuf[1] = jnp.zeros((T, LANES), F32)
        cbuf[1] = jnp.zeros((8 + T, CONV_DIM), F32)

    @pl.when((s == 0) | (i == 0))
    def _init():
        cbuf[prev, 0:8, :] = jnp.zeros((8, CONV_DIM), F32)
        for hp in range(HEAD_PAIRS):
            scr[-1][hp] = jnp.zeros((D_STATE, LANES), F32)

    h = _rms(x_ref[...], nw_ref[...]).astype(BF16)

    def project(w_ref, c0, width, store):
        return lambda: store(jnp.dot(h, w_ref[:, c0:c0 + width].astype(BF16), preferred_element_type=F32))

    def z_store(c0):
        def store(v):
            zbuf[slot, :, c0:c0 + 512] = v
        return store

    def xbc_store(c0):
        def store(v):
            cbuf[slot, 8:8 + T, c0:c0 + 512] = v
        return store

    def cq_store(v):
        cq_ref[...] = v.astype(cq_ref.dtype)

    def dt_store(v):
        dtbuf[slot] = v

    projections = ([project(wmain_ref, c0, 512, z_store(c0)) for c0 in range(0, SSM_INNER, 512)]
                   + [project(wmain_ref, SSM_INNER + c0, 512, xbc_store(c0)) for c0 in range(0, CONV_DIM, 512)]
                   + [project(wside_ref, 0, MEM_WIDTH, cq_store), project(wside_ref, MEM_WIDTH, LANES, dt_store)])
    _ssd_tile(zbuf.at[prev], dtbuf.at[prev], cbuf.at[prev], y_ref, prm, scr, T, projections)
    cbuf[slot, 0:8, :] = cbuf[prev, T:T + 8, :]

    @pl.when((s > 0) & (i == nt - 1))
    def _fin():
        _store_state(hout_ref, scr[-1])
        tail_ref[0] = cbuf[prev, T:T + 8, :]


def _ssm_front(x, p, i_layer, layer, w_main, b, T, h_acc):
    m = x.shape[0]
    nt = m // (b * T)
    n = b * nt
    n_ssm = p["conv_w"].shape[0]
    seq = lambda s: jnp.maximum(s - 1, 0) // nt
    state = pl.BlockSpec((None, 1, SSM_HEADS, SSM_HEAD_DIM, D_STATE), lambda s: (layer, seq(s), 0, 0, 0))
    acc = [] if h_acc is None else [h_acc]
    n_in = 4 + len(SSM_PARAMS) + 1
    return pl.pallas_call(
        functools.partial(_ssm_front_kernel, T=T, nt=nt, has_acc=len(acc)),
        grid=(n + 1,),
        in_specs=[pl.BlockSpec((T, D_MODEL), lambda s: (jnp.minimum(s, n - 1), 0)),
                  _layer(p["norm_mix_pre"], i_layer), _resident(w_main.shape), _layer(p["w_in_side"], layer),
                  *[_layer(p[k], layer) for k in SSM_PARAMS],
                  _resident((3 * LANES, SSM_INNER))] + [pl.BlockSpec(memory_space=pl.ANY) for _ in acc],
        out_specs=[pl.BlockSpec((T, SSM_INNER), lambda s: (jnp.maximum(s - 1, 0), 0)),
                   pl.BlockSpec((T, MEM_WIDTH), lambda s: (jnp.minimum(s, n - 1), 0)),
                   state,
                   pl.BlockSpec((1, 8, CONV_DIM), lambda s: (seq(s), 0, 0))],
        out_shape=[jax.ShapeDtypeStruct((m, SSM_INNER), BF16), jax.ShapeDtypeStruct((m, MEM_WIDTH), BF16),
                   jax.ShapeDtypeStruct((n_ssm, b, SSM_HEADS, SSM_HEAD_DIM, D_STATE), F32),
                   jax.ShapeDtypeStruct((b, 8, CONV_DIM), F32)],
        input_output_aliases={n_in: 2} if acc else {},
        scratch_shapes=[pltpu.VMEM((2, T, SSM_INNER), F32), pltpu.VMEM((2, T, LANES), F32),
                        pltpu.VMEM((2, 8 + T, CONV_DIM), F32)] + _ssd_scratch(),
        compiler_params=pltpu.CompilerParams(dimension_semantics=("arbitrary",), vmem_limit_bytes=VMEM_LIMIT),
        name="ssm_front",
    )(x, p["norm_mix_pre"], w_main, p["w_in_side"], *[p[k] for k in SSM_PARAMS], _head_expand(), *acc)


def _tail_kernel(a_ref, cq_ref, mk_ref, mv_ref, x_ref, wout_ref, npost_ref, nfpre_ref, wup_ref, wdn_ref, nfpost_ref,
                 *refs, nb, T):
    n_cast = len(refs) // 2
    o_ref = refs[n_cast]
    for src, dst in zip(refs[:n_cast], refs[n_cast + 1:]):
        dst[...] = src[...].astype(dst.dtype)
    m = nb * T
    cs = []
    for n in range(nb):
        outs = []
        for h in range(MEM_HEADS):
            hs = slice(h * MEM_HEAD_DIM, (h + 1) * MEM_HEAD_DIM)
            head_rows = pl.ds(h, N_MEM, stride=MEM_HEADS)
            mk = mk_ref[n, head_rows, :].astype(BF16)
            mv = mv_ref[n, head_rows, :].astype(BF16)
            s = lax.dot_general(cq_ref[n, :, hs], mk, NT_DIMS, preferred_element_type=F32)
            s = s * (1.0 / math.sqrt(MEM_HEAD_DIM))
            p = jnp.exp(s - jnp.max(s, axis=-1, keepdims=True))
            den = jnp.sum(p, axis=-1, keepdims=True)
            o = jnp.dot(p.astype(BF16), mv, preferred_element_type=F32) * (1.0 / den)
            outs.append(o.astype(BF16))
        cs.append(jnp.concatenate(outs, axis=1))
    c = cs[0] if nb == 1 else jnp.concatenate(cs, axis=0)
    a = a_ref[...].reshape(m, Q_WIDTH)
    mix = (jnp.dot(a, wout_ref[0:Q_WIDTH, :], preferred_element_type=F32)
           + jnp.dot(c, wout_ref[Q_WIDTH:, :], preferred_element_type=F32))
    x1 = x_ref[...].reshape(m, D_MODEL) + _rms(mix, npost_ref[...])
    h2 = _rms(x1, nfpre_ref[...]).astype(BF16)
    acc = None
    for f0 in range(0, D_FF, 1024):
        u = jnp.maximum(jnp.dot(h2, wup_ref[:, f0:f0 + 1024], preferred_element_type=F32), 0.0)
        d = jnp.dot((u * u).astype(BF16), wdn_ref[f0:f0 + 1024, :], preferred_element_type=F32)
        acc = d if acc is None else acc + d
    o_ref[...] = (x1 + _rms(acc, nfpost_ref[...])).reshape(nb, T, D_MODEL)


def _tail(a, cq, mk, mv, x, p, layer, w_out, w_up, w_down, nb, T, cast=()):
    b, l, _ = x.shape
    steps = (b // nb) * (l // T)
    tok = lambda w: pl.BlockSpec((nb, T, w), lambda bi, i: (bi, i, 0))
    mem = pl.BlockSpec((None, nb, N_MEM * MEM_HEADS, MEM_HEAD_DIM), lambda bi, i: (layer, bi, 0, 0))
    norm = lambda k: _layer(p[k], layer)
    slab = lambda bi, i: bi * (l // T) + i
    cast_in = [pl.BlockSpec((None, w.shape[1] // steps, w.shape[2]), lambda bi, i, lw=lw: (lw, slab(bi, i), 0))
               for w, lw in cast]
    cast_out = [pl.BlockSpec((w.shape[1] // steps, w.shape[2]), lambda bi, i: (slab(bi, i), 0)) for w, _ in cast]
    return pl.pallas_call(
        functools.partial(_tail_kernel, nb=nb, T=T),
        grid=(b // nb, l // T),
        in_specs=[tok(Q_WIDTH), tok(MEM_WIDTH), mem, mem, tok(D_MODEL),
                  _resident(w_out.shape), norm("norm_mix_post"), norm("norm_ffn_pre"), _resident(w_up.shape),
                  _resident(w_down.shape), norm("norm_ffn_post")] + cast_in,
        out_specs=[tok(D_MODEL)] + cast_out,
        out_shape=[jax.ShapeDtypeStruct((b, l, D_MODEL), F32)]
        + [jax.ShapeDtypeStruct(w.shape[1:], BF16) for w, _ in cast],
        compiler_params=pltpu.CompilerParams(dimension_semantics=("parallel", "parallel"),
                                             vmem_limit_bytes=VMEM_LIMIT),
        name="xattn_outproj_mlp",
    )(a, cq, mk, mv, x, w_out, p["norm_mix_post"], p["norm_ffn_pre"], w_up, w_down, p["norm_ffn_post"],
      *[w for w, _ in cast])


ATTN_SEGS = ((0, Q_WIDTH), (Q_WIDTH, 2 * KV_WIDTH), (Q_WIDTH + 2 * KV_WIDTH, MEM_WIDTH))
SSM_MAIN_SEGS = ((0, SSM_INNER), (SSM_INNER, CONV_DIM))
SSM_SIDE_SEGS = ((0, MEM_WIDTH), (MEM_WIDTH, LANES))


def _prep_weights(norm_mix_pre, norm_mix_post, norm_ffn_pre, norm_ffn_post, attn_sinks, w_in_ssm, conv_w,
                  conv_b, dt_bias, a_log, d_skip, ssm_norm):
    dt_lo = SSM_INNER + CONV_DIM
    dt_hi = dt_lo + SSM_HEADS
    pad = jnp.zeros((w_in_ssm.shape[0], D_MODEL, LANES - SSM_HEADS), F32)
    w_side = jnp.concatenate([w_in_ssm[..., dt_hi:], w_in_ssm[..., dt_lo:dt_hi], pad], axis=-1)
    vec = lambda v: v.astype(F32).reshape(v.shape[0], 1, -1)
    lane_pad = lambda v: vec(jnp.pad(v, ((0, 0), (0, LANES - SSM_HEADS))))
    return dict(
        norm_mix_pre=vec(norm_mix_pre), norm_mix_post=vec(norm_mix_post), norm_ffn_pre=vec(norm_ffn_pre),
        norm_ffn_post=vec(norm_ffn_post), w_in_side=w_side,
        attn_bias=jnp.stack([_attn_bias_table(attn_sinks[j]) for j in range(attn_sinks.shape[0])]),
        conv_w=conv_w.astype(F32), conv_b=vec(conv_b), dt_bias=lane_pad(dt_bias), a_log=lane_pad(a_log),
        d_skip=vec(jnp.repeat(d_skip, SSM_HEAD_DIM, axis=-1)), ssm_norm=vec(ssm_norm))


BIG_WEIGHTS = ("w_in", "w_out", "w_up", "w_down")


def _trunk(x, mem_k, mem_v, swa_prev, ssm_h0, conv_hist, p, big, raw, *, tm, attn_tile, ssd_tile, tail_tile):
    b, l, _ = x.shape
    m = b * l
    new_kv, new_conv, states = [], [], None
    for i in range(DEPTH):
        j = i // 2
        xf = x.reshape(m, D_MODEL)
        if i % 2 == 0:
            q, kv, cq = _norm_matmul(xf, p["norm_mix_pre"], i, [(big["w_in"][i], ATTN_SEGS, (BF16, F32, BF16))], j, tm)
            kv = kv.reshape(b, l, 2 * KV_WIDTH)
            a = _attention(q.reshape(b, l, Q_WIDTH), kv, swa_prev, p["attn_bias"], j, *attn_tile)
            new_kv.append(kv)
        elif ssm_h0 is None:
            a, cq, states, conv_tail = _ssm_front(xf, p, i, j, big["w_in"][i], b, ssd_tile, states)
            a = a.reshape(b, l, SSM_INNER)
            new_conv.append(conv_tail[:, 8 - (CONV_W - 1):])
        else:
            z, xbc, cq, dt = _norm_matmul(xf, p["norm_mix_pre"], i,
                                          [(big["w_in"][i], SSM_MAIN_SEGS, (F32, F32)),
                                           (p["w_in_side"], SSM_SIDE_SEGS, (BF16, F32))], j, tm)
            xbc = xbc.reshape(b, l, CONV_DIM)
            a, states = _ssd(z.reshape(b, l, SSM_INNER), xbc, dt.reshape(b, l, LANES), conv_hist, ssm_h0, p, j,
                             ssd_tile, states)
            new_conv.append(xbc[:, l - (CONV_W - 1):])
        cast = [raw[k][i + 1] for k in BIG_WEIGHTS] if raw is not None and i + 1 < DEPTH else []
        x, *copies = _tail(a, cq.reshape(b, l, MEM_WIDTH), mem_k, mem_v, x, p, i, big["w_out"][i], big["w_up"][i],
                           big["w_down"][i], *tail_tile, cast)
        for k, w in zip(BIG_WEIGHTS, copies):
            big[k][i + 1] = w
    return x, new_kv, states, jnp.stack(new_conv)


def kernel(x_prompt, x_sample, mem_prompt, cache_swa_k, cache_swa_v, state_ssm, state_conv, cache_mem_k, cache_mem_v,
           norm_mix_pre, norm_mix_post, norm_ffn_pre, norm_ffn_post, w_in_attn, attn_sinks, w_in_ssm, conv_w, conv_b,
           dt_bias, a_log, d_skip, ssm_norm, mem_norm, w_mem_kv, w_out, w_up, w_down):
    p = _prep_weights(norm_mix_pre, norm_mix_post, norm_ffn_pre, norm_ffn_post, attn_sinks, w_in_ssm, conv_w, conv_b,
                      dt_bias, a_log, d_skip, ssm_norm)
    bp, lp, _ = x_prompt.shape
    bs, ls, _ = x_sample.shape
    n_attn = w_in_attn.shape[0]
    raw = dict(w_in=[(w_in_ssm if i % 2 else w_in_attn, i // 2) for i in range(DEPTH)],
               w_out=[(w_out, i) for i in range(DEPTH)], w_up=[(w_up, i) for i in range(DEPTH)],
               w_down=[(w_down, i) for i in range(DEPTH)])
    big = {k: [raw[k][0][0][0].astype(BF16)] + [None] * (DEPTH - 1) for k in BIG_WEIGHTS}

    mem_rows = N_MEM * MEM_HEADS
    prompt_mem_k, prompt_mem_v = _mem_kv(mem_prompt.reshape(bp * N_MEM, D_MODEL),
                                         mem_norm.astype(F32).reshape(DEPTH, 1, D_MODEL), w_mem_kv.astype(BF16), 512)

    y_prompt, kv_p, prompt_ssm, prompt_conv = _trunk(
        x_prompt, prompt_mem_k.reshape(DEPTH, bp, mem_rows, MEM_HEAD_DIM),
        prompt_mem_v.reshape(DEPTH, bp, mem_rows, MEM_HEAD_DIM), None, None, None, p, big, raw,
        tm=1024, attn_tile=(1, 256), ssd_tile=256, tail_tile=(1, 512))

    keep = cache_swa_k.shape[2]
    swa_prev = jnp.concatenate([cache_swa_k.reshape(n_attn, bs, keep, KV_WIDTH),
                                cache_swa_v.reshape(n_attn, bs, keep, KV_WIDTH)], axis=-1)
    hist = jnp.pad(state_conv, ((0, 0), (0, 0), (8 - (CONV_W - 1), 0), (0, 0)))
    y_sample, kv_s, sample_ssm, sample_conv = _trunk(
        x_sample, cache_mem_k.reshape(DEPTH, bs, mem_rows, MEM_HEAD_DIM),
        cache_mem_v.reshape(DEPTH, bs, mem_rows, MEM_HEAD_DIM), swa_prev, state_ssm, hist, p, big, None,
        tm=512, attn_tile=(4, ls), ssd_tile=ls, tail_tile=(8, ls))

    def split_kv(kvs, rows):
        kv = jnp.stack([kv[:, -rows:] for kv in kvs])
        shape = kv.shape[:3] + (N_KV_HEADS, HEAD_DIM)
        return kv[..., :KV_WIDTH].reshape(shape), kv[..., KV_WIDTH:].reshape(shape)

    prompt_swa_k, prompt_swa_v = split_kv(kv_p, min(WINDOW, lp))
    sample_swa_k, sample_swa_v = split_kv(kv_s, ls)
    mem_shape = (DEPTH, bp, N_MEM, MEM_HEADS, MEM_HEAD_DIM)
    return (y_prompt, y_sample, prompt_swa_k, prompt_swa_v, prompt_ssm, prompt_conv,
            prompt_mem_k.reshape(mem_shape), prompt_mem_v.reshape(mem_shape),
            sample_swa_k, sample_swa_v, sample_ssm, sample_conv)
```

```python
import functools
import math

import numpy as np
import jax
import jax.numpy as jnp
from jax import lax
from jax.experimental import pallas as pl
from jax.experimental.pallas import tpu as pltpu

F32 = jnp.float32
BF16 = jnp.bfloat16

D_MODEL = 1024
DEPTH = 4
CHUNK = 64
N_HEADS = 24
N_KV_HEADS = 3
HEAD_DIM = 64
WINDOW = 128
Q_WIDTH = N_HEADS * HEAD_DIM
KV_WIDTH = N_KV_HEADS * HEAD_DIM
N_MEM = 256
MEM_HEADS = 4
MEM_HEAD_DIM = 128
MEM_WIDTH = MEM_HEADS * MEM_HEAD_DIM
SSM_INNER = 1536
SSM_HEAD_DIM = 64
SSM_HEADS = SSM_INNER // SSM_HEAD_DIM
SSM_GROUPS = 8
SSM_GROUP_WIDTH = SSM_INNER // SSM_GROUPS
D_STATE = 128
CONV_W = 4
CONV_DIM = SSM_INNER + 2 * SSM_GROUPS * D_STATE
D_FF = 4 * D_MODEL
EPS = 1e-6

LANES = 128
HEAD_PAIRS = SSM_HEADS // 2
NEG = -1e30
TRANSPOSE_SLAB = 2 * LANES
VMEM_LIMIT = 56 * 1024 * 1024

NT_DIMS = (((1,), (1,)), ((), ()))


def _rms(x, w):
    return x * lax.rsqrt(jnp.mean(x * x, axis=-1, keepdims=True) + EPS) * w


def _sigmoid(x):
    return 1.0 / (1.0 + jnp.exp(-x))


def _split3(x):
    a = x.astype(BF16)
    r = x - a.astype(F32)
    b = r.astype(BF16)
    c = (r - b.astype(F32)).astype(BF16)
    return a, b, c


def _resident(shape):
    return pl.BlockSpec(shape, lambda *_: (0,) * len(shape), pipeline_mode=pl.Buffered(1))


def _layer(stacked, layer):
    shape = stacked.shape[1:]
    return pl.BlockSpec((None,) + shape, lambda *_: (layer,) + (0,) * len(shape), pipeline_mode=pl.Buffered(1))


def _norm_matmul_kernel(x_ref, nw_ref, *refs, plan):
    w_refs, o_refs = refs[:len(plan)], list(refs[len(plan):])
    h = _rms(x_ref[...], nw_ref[...]).astype(BF16)
    for w_ref, segs in zip(w_refs, plan):
        for start, width in segs:
            o_ref = o_refs.pop(0)
            for c0 in range(0, width, 512):
                cw = min(512, width - c0)
                o_ref[:, c0:c0 + cw] = jnp.dot(
                    h, w_ref[:, start + c0:start + c0 + cw].astype(BF16), preferred_element_type=F32).astype(o_ref.dtype)


def _norm_matmul(x, nw, nw_layer, weights, w_layer, tm):
    m, k = x.shape
    plan = tuple(segs for _, segs, _ in weights)
    widths = [wd for _, segs, _ in weights for _, wd in segs]
    dtypes = [dt for _, _, dts in weights for dt in dts]
    return pl.pallas_call(
        functools.partial(_norm_matmul_kernel, plan=plan),
        grid=(m // tm,),
        in_specs=([pl.BlockSpec((tm, k), lambda i: (i, 0)), _layer(nw, nw_layer)]
                  + [_layer(w, w_layer) if w.ndim == 3 else _resident(w.shape) for w, _, _ in weights]),
        out_specs=[pl.BlockSpec((tm, wd), lambda i: (i, 0)) for wd in widths],
        out_shape=[jax.ShapeDtypeStruct((m, wd), dt) for wd, dt in zip(widths, dtypes)],
        compiler_params=pltpu.CompilerParams(dimension_semantics=("parallel",), vmem_limit_bytes=VMEM_LIMIT),
        name="norm_matmul",
    )(x, nw, *[w for w, _, _ in weights])


def _mem_kv_kernel(x_ref, nw_ref, w_ref, k_ref, v_ref):
    tm = x_ref.shape[0]
    h = _rms(x_ref[...], nw_ref[...]).astype(BF16)
    for o_ref, c0 in ((k_ref, 0), (v_ref, MEM_WIDTH)):
        r = jnp.dot(h, w_ref[:, c0:c0 + MEM_WIDTH], preferred_element_type=F32)
        for hd in range(MEM_HEADS):
            o_ref[pl.ds(hd, tm, stride=MEM_HEADS), :] = r[:, hd * MEM_HEAD_DIM:(hd + 1) * MEM_HEAD_DIM]


def _mem_kv(mem, nw, w, tm):
    m, k = mem.shape
    out = jax.ShapeDtypeStruct((DEPTH, m * MEM_HEADS, MEM_HEAD_DIM), F32)
    out_spec = pl.BlockSpec((None, tm * MEM_HEADS, MEM_HEAD_DIM), lambda l, i: (l, i, 0))
    return pl.pallas_call(
        _mem_kv_kernel,
        grid=(DEPTH, m // tm),
        in_specs=[pl.BlockSpec((tm, k), lambda l, i: (i, 0)),
                  pl.BlockSpec((None, 1, k), lambda l, i: (l, 0, 0)),
                  pl.BlockSpec((None, k, 2 * MEM_WIDTH), lambda l, i: (l, 0, 0))],
        out_specs=[out_spec, out_spec],
        out_shape=[out, out],
        compiler_params=pltpu.CompilerParams(dimension_semantics=("parallel", "parallel"),
                                             vmem_limit_bytes=VMEM_LIMIT),
        name="mem_kv",
    )(mem, nw, w)


def _attn_bias_table(sinks):
    slopes = jnp.asarray(2.0 ** (-8.0 * np.arange(1, N_HEADS + 1) / N_HEADS), dtype=F32).reshape(N_KV_HEADS, 4, 1, 2, 1)
    span = WINDOW + CHUNK
    i = np.arange(CHUNK)[:, None]
    s = np.arange(2 * LANES)[None, :]
    dist = jnp.asarray(np.abs(WINDOW + i - s), dtype=F32)[None, None, :, None, :]
    slot = jnp.asarray(s)[None, None, :, None, :]
    bias = -slopes * dist
    bias = jnp.where(slot < span, bias, NEG)
    bias = jnp.where(slot == span, sinks.astype(F32).reshape(N_KV_HEADS, 4, 1, 2, 1), bias)
    return bias.reshape(N_KV_HEADS, 4 * CHUNK, 4 * LANES)


def _attn_kernel(q_ref, kvc_ref, kvp_ref, bias_ref, o_ref, kvbuf, *, nb, T, mask_first):
    first = pl.program_id(1) == 0
    lo = lax.broadcasted_iota(jnp.int32, (1, LANES), 1) < HEAD_DIM
    slot = lax.broadcasted_iota(jnp.int32, (1, 4 * LANES), 1) & (2 * LANES - 1)
    zpad = jnp.zeros((CHUNK, LANES), BF16)
    ones_halves = jnp.where(
        (lax.broadcasted_iota(jnp.int32, (4 * LANES, LANES), 0) < 2 * LANES)
        == (lax.broadcasted_iota(jnp.int32, (4 * LANES, LANES), 1) < HEAD_DIM), 1.0, 0.0).astype(BF16)
    span = WINDOW + CHUNK
    for n in range(nb):
        kvbuf[0:WINDOW, :] = kvp_ref[n]
        kvbuf[WINDOW:WINDOW + T, :] = kvc_ref[n]
        for j in range(T // CHUNK):
            win = kvbuf[j * CHUNK:j * CHUNK + span, :]
            blocks = [win[:, m * LANES:(m + 1) * LANES] for m in range(3)]
            rolled = [pltpu.roll(b, HEAD_DIM, 1) for b in blocks]

            def halves(idx):
                m, half = divmod(idx, 2)
                in_lo, in_hi = (blocks[m], rolled[m]) if half == 0 else (rolled[m], blocks[m])
                return jnp.where(lo, in_lo, 0.0).astype(BF16), jnp.where(lo, 0.0, in_hi).astype(BF16)

            for kvh in range(N_KV_HEADS):
                klo, khi = halves(kvh)
                vlo, vhi = halves(N_KV_HEADS + kvh)
                kbd = jnp.concatenate([klo, zpad, khi, zpad], axis=0)
                c0 = kvh * 4 * LANES
                qs = jnp.concatenate(
                    [q_ref[n, j * CHUNK:(j + 1) * CHUNK, c0 + p * LANES:c0 + (p + 1) * LANES] for p in range(4)],
                    axis=0)
                s = lax.dot_general(qs * (1.0 / math.sqrt(HEAD_DIM)), kbd, NT_DIMS, preferred_element_type=F32)
                s = s + bias_ref[kvh]
                if mask_first and j < WINDOW // CHUNK:
                    s = jnp.where(slot < jnp.where(first, WINDOW - j * CHUNK, 0), NEG, s)
                ps = []
                for e in range(2):
                    se = s[:, e * 2 * LANES:(e + 1) * 2 * LANES]
                    ps.append(jnp.exp(se - jnp.max(se, axis=-1, keepdims=True)).astype(BF16))
                od = jnp.dot(jnp.concatenate(ps, axis=1),
                             jnp.concatenate([jnp.concatenate([vlo, zpad, vhi, zpad], axis=0), ones_halves], axis=1),
                             preferred_element_type=F32)
                o = od[:, :LANES] / od[:, LANES:]
                for p in range(4):
                    o_ref[n, j * CHUNK:(j + 1) * CHUNK, c0 + p * LANES:c0 + (p + 1) * LANES] = (
                        o[p * CHUNK:(p + 1) * CHUNK].astype(o_ref.dtype))


def _attention(q, kv, kv_prev, bias, layer, nb, T):
    b, l, _ = q.shape
    mask_first = kv_prev is None
    if mask_first:
        prev_arr = kv
        prev_spec = pl.BlockSpec((nb, WINDOW, 2 * KV_WIDTH),
                                 lambda bi, i: (bi, jnp.maximum(i * (T // WINDOW) - 1, 0), 0))
    else:
        prev_arr = kv_prev
        prev_spec = pl.BlockSpec((None, nb, WINDOW, 2 * KV_WIDTH), lambda bi, i: (layer, bi, 0, 0))
    return pl.pallas_call(
        functools.partial(_attn_kernel, nb=nb, T=T, mask_first=mask_first),
        grid=(b // nb, l // T),
        in_specs=[pl.BlockSpec((nb, T, Q_WIDTH), lambda bi, i: (bi, i, 0)),
                  pl.BlockSpec((nb, T, 2 * KV_WIDTH), lambda bi, i: (bi, i, 0)),
                  prev_spec,
                  _layer(bias, layer)],
        out_specs=pl.BlockSpec((nb, T, Q_WIDTH), lambda bi, i: (bi, i, 0)),
        out_shape=jax.ShapeDtypeStruct((b, l, Q_WIDTH), BF16),
        scratch_shapes=[pltpu.VMEM((WINDOW + T, 2 * KV_WIDTH), F32)],
        compiler_params=pltpu.CompilerParams(dimension_semantics=("parallel", "parallel"),
                                             vmem_limit_bytes=VMEM_LIMIT),
        name="swa_attention",
    )(q, kv, prev_arr, bias)


def _load_state(st, h0_ref):
    for hp in range(HEAD_PAIRS):
        st[hp] = h0_ref[0, 2 * hp:2 * hp + 2].reshape(2 * SSM_HEAD_DIM, D_STATE).T


def _store_state(hout_ref, st, prev_ref):
    n_prev = hout_ref.shape[0] - 1
    if prev_ref is not None:
        hout_ref[0:n_prev] = prev_ref[...]
    for hp in range(HEAD_PAIRS):
        hout_ref[n_prev, 0, 2 * hp:2 * hp + 2] = st[hp].T.reshape(2, SSM_HEAD_DIM, D_STATE)


def _ssd_kernel(z_ref, xbc_ref, dt_ref, hist_ref, h0_ref, *refs, T, has_prev):
    prm, prev_ref, refs = refs[:7], refs[7] if has_prev else None, refs[7 + has_prev:]
    (y_ref, hout_ref, cbuf), scr = refs[:3], refs[3:]
    i = pl.program_id(1)

    @pl.when(i == 0)
    def _init():
        cbuf[0:8, :] = hist_ref[0]
        _load_state(scr[-1], h0_ref)

    cbuf[8:8 + T, :] = xbc_ref[0]
    _ssd_tile(z_ref.at[0], dt_ref.at[0], cbuf, y_ref.at[0], prm, scr, T)
    cbuf[0:8, :] = cbuf[T:T + 8, :]

    @pl.when(i == pl.num_programs(1) - 1)
    def _fin():
        _store_state(hout_ref, scr[-1], prev_ref)


def _ssd_tile(z_ref, dt_ref, cbuf, y_ref, prm, scr, T, side_work=()):
    for work in side_work:
        work()
    cw_ref, cb_ref, dtb_ref, alog_ref, dsk_ref, nw_ref, e64_ref = prm
    ubuf, ybuf, dtx_s, acs_s, st = scr
    row = lax.broadcasted_iota(jnp.int32, (CHUNK, LANES), 0)
    pos = lax.broadcasted_iota(jnp.int32, (CHUNK, LANES), 1) & (CHUNK - 1)
    diag = pos == row
    causal = pos <= row
    lo = lax.broadcasted_iota(jnp.int32, (1, LANES), 1) < SSM_HEAD_DIM
    tri = (lax.broadcasted_iota(jnp.int32, (CHUNK, CHUNK), 1)
           <= lax.broadcasted_iota(jnp.int32, (CHUNK, CHUNK), 0)).astype(F32).astype(BF16)
    a_neg = -jnp.exp(alog_ref[...])
    b_off = SSM_INNER
    c_off = SSM_INNER + SSM_GROUPS * D_STATE

    for j in range(T // CHUNK):
        r0 = 8 + j * CHUNK
        rows = slice(j * CHUNK, (j + 1) * CHUNK)
        for c0 in range(0, CONV_DIM, LANES):
            cols = slice(c0, c0 + LANES)
            xw = cbuf[r0 - 8:r0 + CHUNK, cols]
            xw1 = pltpu.roll(xw, 1, 0)
            near = xw * cw_ref[3:4, cols] + xw1 * cw_ref[2:3, cols]
            far = pltpu.roll(xw * cw_ref[1:2, cols] + xw1 * cw_ref[0:1, cols], 2, 0)
            acc = (near + far)[8:] + cb_ref[:, cols]
            ubuf[:, cols] = acc * _sigmoid(acc)

        dt_in = dt_ref[rows, :] + dtb_ref[...]
        dtv = jnp.maximum(dt_in, 0.0) + jnp.log1p(jnp.exp(-jnp.abs(dt_in)))
        cum3 = jnp.dot(tri, jnp.concatenate(_split3(dtv * a_neg), axis=1), preferred_element_type=F32)
        cum = cum3[:, 2 * LANES:] + cum3[:, LANES:2 * LANES] + cum3[:, :LANES]
        terms = jnp.concatenate([jnp.concatenate(_split3(dtv), axis=1), jnp.concatenate(_split3(cum), axis=1)], axis=0)
        both = jnp.dot(terms, e64_ref[...], preferred_element_type=F32)
        dtx_s[...] = both[:CHUNK]
        acs_s[...] = both[CHUNK:]

        bb, bt, cb, cbd = [], [], [], []
        for g in range(SSM_GROUPS):
            bg = ubuf[:, b_off + g * D_STATE:b_off + (g + 1) * D_STATE]
            bb.append(bg.astype(BF16))
            bt.append(bg.T.astype(BF16))
            cb.append(ubuf[:, c_off + g * D_STATE:c_off + (g + 1) * D_STATE].astype(BF16))
            cbd.append(lax.dot_general(cb[g], jnp.concatenate([bb[g], bb[g]], axis=0), NT_DIMS,
                                       preferred_element_type=F32))

        for hp in range(HEAD_PAIRS):
            sl = slice(hp * LANES, (hp + 1) * LANES)
            g0, g1 = (2 * hp) // 3, (2 * hp + 1) // 3
            acs = acs_s[:, sl]
            acs_key = jnp.sum(jnp.where(diag, acs, 0.0), axis=0, keepdims=True)
            lmat = jnp.exp(jnp.where(causal, acs - acs_key, NEG))
            eacs = jnp.exp(acs)
            acs_last = acs[CHUNK - 1:CHUNK, :]
            xp = ubuf[:, sl]
            xdt = xp * dtx_s[:, sl]
            xlo = jnp.where(lo, xdt, 0.0)
            xhi = jnp.where(lo, 0.0, xdt)

            cbp = cbd[g0] if g0 == g1 else jnp.where(lo, cbd[g0], cbd[g1])
            xbd = jnp.concatenate([xlo, xhi], axis=0).astype(BF16)
            yd = jnp.dot((cbp * lmat).astype(BF16), xbd, preferred_element_type=F32)

            ps = st[hp]
            psb = ps.astype(BF16)
            yo = jnp.dot(cb[g0], psb, preferred_element_type=F32)
            if g0 != g1:
                yo = jnp.where(lo, yo, jnp.dot(cb[g1], psb, preferred_element_type=F32))
            ybuf[:, sl] = yd + yo * eacs + dsk_ref[:, sl] * xp

            wx = jnp.exp(acs_last - acs)
            if g0 == g1:
                new = jnp.dot(bt[g0], (xdt * wx).astype(BF16), preferred_element_type=F32)
            else:
                new = (jnp.dot(bt[g0], (xlo * wx).astype(BF16), preferred_element_type=F32)
                       + jnp.dot(bt[g1], (xhi * wx).astype(BF16), preferred_element_type=F32))
            st[hp] = ps * eacs[CHUNK - 1:CHUNK, :] + new

        lane = lax.broadcasted_iota(jnp.int32, (1, LANES), 1)
        blocks = []
        for v in range(SSM_INNER // LANES):
            ga, gb = (v * LANES) // SSM_GROUP_WIDTH, (v * LANES + LANES - 1) // SSM_GROUP_WIDTH
            blocks.append((slice(v * LANES, (v + 1) * LANES), ga, gb, lane < gb * SSM_GROUP_WIDTH - v * LANES))
        ss = [None] * SSM_GROUPS
        for vs, ga, gb, in_ga in blocks:
            zv = z_ref[rows, vs]
            yv = ybuf[:, vs] * (zv * _sigmoid(zv))
            ybuf[:, vs] = yv
            y2 = yv * yv
            parts = [(ga, y2)] if ga == gb else [(ga, jnp.where(in_ga, y2, 0.0)), (gb, jnp.where(in_ga, 0.0, y2))]
            for g, part in parts:
                t = jnp.sum(part, axis=-1, keepdims=True)
                ss[g] = t if ss[g] is None else ss[g] + t
        rs = [lax.rsqrt(t / SSM_GROUP_WIDTH + EPS) for t in ss]
        for vs, ga, gb, in_ga in blocks:
            sc = rs[ga] if ga == gb else jnp.where(in_ga, rs[ga], rs[gb])
            y_ref[rows, vs] = (ybuf[:, vs] * sc * nw_ref[:, vs]).astype(y_ref.dtype)


SSM_PARAMS = ("conv_w", "conv_b", "dt_bias", "a_log", "d_skip", "ssm_norm")


def _head_expand():
    one = np.arange(LANES)[:, None] == (np.arange(SSM_INNER)[None, :] // SSM_HEAD_DIM)
    return jnp.asarray(np.tile(one, (3, 1)), dtype=BF16)


def _ssd_scratch():
    return [pltpu.VMEM((CHUNK, CONV_DIM), F32), pltpu.VMEM((CHUNK, SSM_INNER), F32),
            pltpu.VMEM((CHUNK, SSM_INNER), F32), pltpu.VMEM((CHUNK, SSM_INNER), F32),
            pltpu.VMEM((HEAD_PAIRS, D_STATE, LANES), F32)]


def _state_specs(prev, layer, b, seq):
    shape = (SSM_HEADS, SSM_HEAD_DIM, D_STATE)
    spec = lambda n: pl.BlockSpec((n, 1) + shape, lambda *g: (0, seq(*g), 0, 0, 0))
    ins = [] if prev is None else [spec(layer)]
    return ins, spec(layer + 1), jax.ShapeDtypeStruct((layer + 1, b) + shape, F32)


def _ssd(z, xbc, dt, hist, h0, p, layer, T, prev):
    b, l, _ = z.shape
    tok = lambda w: pl.BlockSpec((1, T, w), lambda bi, i: (bi, i, 0))
    prev_specs, state_spec, state_shape = _state_specs(prev, layer, b, lambda bi, i: bi)
    return pl.pallas_call(
        functools.partial(_ssd_kernel, T=T, has_prev=len(prev_specs)),
        grid=(b, l // T),
        in_specs=[tok(SSM_INNER), tok(CONV_DIM), tok(LANES),
                  pl.BlockSpec((None, 1, 8, CONV_DIM), lambda bi, i: (layer, bi, 0, 0)),
                  pl.BlockSpec((None, 1, SSM_HEADS, SSM_HEAD_DIM, D_STATE), lambda bi, i: (layer, bi, 0, 0, 0)),
                  *[_layer(p[k], layer) for k in SSM_PARAMS],
                  _resident((3 * LANES, SSM_INNER))] + prev_specs,
        out_specs=[tok(SSM_INNER), state_spec],
        out_shape=[jax.ShapeDtypeStruct((b, l, SSM_INNER), BF16), state_shape],
        scratch_shapes=[pltpu.VMEM((8 + T, CONV_DIM), F32)] + _ssd_scratch(),
        compiler_params=pltpu.CompilerParams(dimension_semantics=("parallel", "arbitrary"),
                                             vmem_limit_bytes=VMEM_LIMIT),
        name="conv_ssd",
    )(z, xbc, dt, hist, h0, *[p[k] for k in SSM_PARAMS], _head_expand(), *([] if prev is None else [prev]))


def _ssm_front_kernel(x_ref, nw_ref, wmain_ref, wside_ref, *refs, T, nt, has_prev):
    prm, prev_ref, refs = refs[:7], refs[7] if has_prev else None, refs[7 + has_prev:]
    (y_ref, cq_ref, hout_ref, tail_ref), (zbuf, dtbuf, cbuf), scr = refs[:4], refs[4:7], refs[7:]
    s = pl.program_id(0)
    slot = lax.rem(s, 2)
    prev = 1 - slot
    i = lax.rem(s + nt - 1, nt)

    @pl.when(s == 0)
    def _first():
        zbuf[1] = jnp.zeros((T, SSM_INNER), F32)
        dtbuf[1] = jnp.zeros((T, LANES), F32)
        cbuf[1] = jnp.zeros((8 + T, CONV_DIM), F32)

    @pl.when((s == 0) | (i == 0))
    def _init():
        cbuf[prev, 0:8, :] = jnp.zeros((8, CONV_DIM), F32)
        for hp in range(HEAD_PAIRS):
            scr[-1][hp] = jnp.zeros((D_STATE, LANES), F32)

    h = _rms(x_ref[...], nw_ref[...]).astype(BF16)

    def project(w_ref, c0, width, store):
        return lambda: store(jnp.dot(h, w_ref[:, c0:c0 + width].astype(BF16), preferred_element_type=F32))

    def z_store(c0):
        def store(v):
            zbuf[slot, :, c0:c0 + 512] = v
        return store

    def xbc_store(c0):
        def store(v):
            cbuf[slot, 8:8 + T, c0:c0 + 512] = v
        return store

    def cq_store(v):
        cq_ref[...] = v.astype(cq_ref.dtype)

    def dt_store(v):
        dtbuf[slot] = v

    projections = ([project(wmain_ref, c0, 512, z_store(c0)) for c0 in range(0, SSM_INNER, 512)]
                   + [project(wmain_ref, SSM_INNER + c0, 512, xbc_store(c0)) for c0 in range(0, CONV_DIM, 512)]
                   + [project(wside_ref, 0, MEM_WIDTH, cq_store), project(wside_ref, MEM_WIDTH, LANES, dt_store)])
    _ssd_tile(zbuf.at[prev], dtbuf.at[prev], cbuf.at[prev], y_ref, prm, scr, T, projections)
    cbuf[slot, 0:8, :] = cbuf[prev, T:T + 8, :]

    @pl.when((s > 0) & (i == nt - 1))
    def _fin():
        _store_state(hout_ref, scr[-1], prev_ref)
        tail_ref[0] = cbuf[prev, T:T + 8, :]


def _ssm_front(x, p, i_layer, layer, w_main, b, T, prev):
    m = x.shape[0]
    nt = m // (b * T)
    n = b * nt
    seq = lambda s: jnp.maximum(s - 1, 0) // nt
    prev_specs, state_spec, state_shape = _state_specs(prev, layer, b, seq)
    return pl.pallas_call(
        functools.partial(_ssm_front_kernel, T=T, nt=nt, has_prev=len(prev_specs)),
        grid=(n + 1,),
        in_specs=[pl.BlockSpec((T, D_MODEL), lambda s: (jnp.minimum(s, n - 1), 0)),
                  _layer(p["norm_mix_pre"], i_layer), _resident(w_main.shape), _layer(p["w_in_side"], layer),
                  *[_layer(p[k], layer) for k in SSM_PARAMS],
                  _resident((3 * LANES, SSM_INNER))] + prev_specs,
        out_specs=[pl.BlockSpec((T, SSM_INNER), lambda s: (jnp.maximum(s - 1, 0), 0)),
                   pl.BlockSpec((T, MEM_WIDTH), lambda s: (jnp.minimum(s, n - 1), 0)),
                   state_spec,
                   pl.BlockSpec((1, 8, CONV_DIM), lambda s: (seq(s), 0, 0))],
        out_shape=[jax.ShapeDtypeStruct((m, SSM_INNER), BF16), jax.ShapeDtypeStruct((m, MEM_WIDTH), BF16),
                   state_shape, jax.ShapeDtypeStruct((b, 8, CONV_DIM), F32)],
        scratch_shapes=[pltpu.VMEM((2, T, SSM_INNER), F32), pltpu.VMEM((2, T, LANES), F32),
                        pltpu.VMEM((2, 8 + T, CONV_DIM), F32)] + _ssd_scratch(),
        compiler_params=pltpu.CompilerParams(dimension_semantics=("arbitrary",), vmem_limit_bytes=VMEM_LIMIT),
        name="ssm_front",
    )(x, p["norm_mix_pre"], w_main, p["w_in_side"], *[p[k] for k in SSM_PARAMS], _head_expand(),
      *([] if prev is None else [prev]))


def _tail_kernel(a_ref, cq_ref, mk_ref, mv_ref, x_ref, wout_ref, npost_ref, nfpre_ref, wup_ref, wdn_ref, nfpost_ref,
                 *refs, nb, T):
    n_cast = len(refs) // 2
    o_ref = refs[n_cast]
    for src, dst in zip(refs[:n_cast], refs[n_cast + 1:]):
        w = src[...]
        dst[...] = (w if w.shape == dst.shape else w.T).astype(dst.dtype)
    m = nb * T
    cs = []
    for n in range(nb):
        outs = []
        for h in range(MEM_HEADS):
            hs = slice(h * MEM_HEAD_DIM, (h + 1) * MEM_HEAD_DIM)
            head_rows = pl.ds(h, N_MEM, stride=MEM_HEADS)
            mk = mk_ref[n, head_rows, :].astype(BF16)
            mv = mv_ref[n, head_rows, :].astype(BF16)
            s = lax.dot_general(cq_ref[n, :, hs], mk, NT_DIMS, preferred_element_type=F32)
            s = s * (1.0 / math.sqrt(MEM_HEAD_DIM))
            p = jnp.exp(s - jnp.max(s, axis=-1, keepdims=True))
            den = jnp.sum(p, axis=-1, keepdims=True)
            o = jnp.dot(p.astype(BF16), mv, preferred_element_type=F32) * (1.0 / den)
            outs.append(o.astype(BF16))
        cs.append(jnp.concatenate(outs, axis=1))
    c = cs[0] if nb == 1 else jnp.concatenate(cs, axis=0)
    a = a_ref[...].reshape(m, Q_WIDTH)
    mix = (jnp.dot(a, wout_ref[0:Q_WIDTH, :], preferred_element_type=F32)
           + jnp.dot(c, wout_ref[Q_WIDTH:, :], preferred_element_type=F32))
    x1 = x_ref[...].reshape(m, D_MODEL) + _rms(mix, npost_ref[...])
    h2 = _rms(x1, nfpre_ref[...]).astype(BF16)
    acc = None
    for f0 in range(0, D_FF, 1024):
        u = jnp.maximum(jnp.dot(h2, wup_ref[:, f0:f0 + 1024], preferred_element_type=F32), 0.0)
        d = jnp.dot((u * u).astype(BF16), wdn_ref[f0:f0 + 1024, :], preferred_element_type=F32)
        acc = d if acc is None else acc + d
    o_ref[...] = (x1 + _rms(acc, nfpost_ref[...])).reshape(nb, T, D_MODEL)


def _tail(a, cq, mk, mv, x, p, layer, w_out, w_up, w_down, nb, T, cast=()):
    b, l, _ = x.shape
    steps = (b // nb) * (l // T)
    tok = lambda w: pl.BlockSpec((nb, T, w), lambda bi, i: (bi, i, 0))
    mem = pl.BlockSpec((None, nb, N_MEM * MEM_HEADS, MEM_HEAD_DIM), lambda bi, i: (layer, bi, 0, 0))
    norm = lambda k: _layer(p[k], layer)
    slab = lambda bi, i: bi * (l // T) + i
    cast_in, cast_out, cast_shape = [], [], []
    for w, lw, cols in cast:
        if cols is None:
            rows = w.shape[1] // steps
            cast_in.append(pl.BlockSpec((None, rows, w.shape[2]), lambda bi, i, lw=lw: (lw, slab(bi, i), 0)))
            cast_out.append(pl.BlockSpec((rows, w.shape[2]), lambda bi, i: (slab(bi, i), 0)))
            cast_shape.append(w.shape[1:])
        else:
            last = -(-cols // TRANSPOSE_SLAB) - 1
            cast_in.append(pl.BlockSpec((None, TRANSPOSE_SLAB, w.shape[2]),
                                        lambda bi, i, lw=lw, last=last: (lw, jnp.minimum(slab(bi, i), last), 0)))
            cast_out.append(pl.BlockSpec((w.shape[2], TRANSPOSE_SLAB),
                                         lambda bi, i, last=last: (0, jnp.minimum(slab(bi, i), last))))
            cast_shape.append((w.shape[2], cols))
    return pl.pallas_call(
        functools.partial(_tail_kernel, nb=nb, T=T),
        grid=(b // nb, l // T),
        in_specs=[tok(Q_WIDTH), tok(MEM_WIDTH), mem, mem, tok(D_MODEL),
                  _resident(w_out.shape), norm("norm_mix_post"), norm("norm_ffn_pre"), _resident(w_up.shape),
                  _resident(w_down.shape), norm("norm_ffn_post")] + cast_in,
        out_specs=[tok(D_MODEL)] + cast_out,
        out_shape=[jax.ShapeDtypeStruct((b, l, D_MODEL), F32)]
        + [jax.ShapeDtypeStruct(shape, BF16) for shape in cast_shape],
        compiler_params=pltpu.CompilerParams(dimension_semantics=("arbitrary", "arbitrary"),
                                             vmem_limit_bytes=VMEM_LIMIT),
        name="xattn_outproj_mlp",
    )(a, cq, mk, mv, x, w_out, p["norm_mix_post"], p["norm_ffn_pre"], w_up, w_down, p["norm_ffn_post"],
      *[w for w, _, _ in cast])


ATTN_SEGS = ((0, Q_WIDTH), (Q_WIDTH, 2 * KV_WIDTH), (Q_WIDTH + 2 * KV_WIDTH, MEM_WIDTH))
SSM_MAIN_SEGS = ((0, SSM_INNER), (SSM_INNER, CONV_DIM))
SSM_SIDE_SEGS = ((0, MEM_WIDTH), (MEM_WIDTH, LANES))


def _prep_weights(norm_mix_pre, norm_mix_post, norm_ffn_pre, norm_ffn_post, attn_sinks, w_in_ssm, conv_w,
                  conv_b, dt_bias, a_log, d_skip, ssm_norm):
    dt_lo = SSM_INNER + CONV_DIM
    dt_hi = dt_lo + SSM_HEADS
    pad = jnp.zeros((w_in_ssm.shape[0], D_MODEL, LANES - SSM_HEADS), F32)
    w_side = jnp.concatenate([w_in_ssm[..., dt_hi:], w_in_ssm[..., dt_lo:dt_hi], pad], axis=-1)
    vec = lambda v: v.astype(F32).reshape(v.shape[0], 1, -1)
    lane_pad = lambda v: vec(jnp.pad(v, ((0, 0), (0, LANES - SSM_HEADS))))
    return dict(
        norm_mix_pre=vec(norm_mix_pre), norm_mix_post=vec(norm_mix_post), norm_ffn_pre=vec(norm_ffn_pre),
        norm_ffn_post=vec(norm_ffn_post), w_in_side=w_side,
        attn_bias=jnp.stack([_attn_bias_table(attn_sinks[j]) for j in range(attn_sinks.shape[0])]),
        conv_w=conv_w.astype(F32), conv_b=vec(conv_b), dt_bias=lane_pad(dt_bias), a_log=lane_pad(a_log),
        d_skip=vec(jnp.repeat(d_skip, SSM_HEAD_DIM, axis=-1)), ssm_norm=vec(ssm_norm))


BIG_WEIGHTS = ("w_in", "w_out", "w_up", "w_down")


def _trunk(x, mem_k, mem_v, swa_prev, ssm_h0, conv_hist, p, big, raw, *, tm, attn_tile, ssd_tile, tail_tile):
    b, l, _ = x.shape
    m = b * l
    new_kv, new_conv, states = [], [], None
    for i in range(DEPTH):
        j = i // 2
        xf = x.reshape(m, D_MODEL)
        if i % 2 == 0:
            q, kv, cq = _norm_matmul(xf, p["norm_mix_pre"], i, [(big["w_in"][i], ATTN_SEGS, (BF16, F32, BF16))], j, tm)
            kv = kv.reshape(b, l, 2 * KV_WIDTH)
            a = _attention(q.reshape(b, l, Q_WIDTH), kv, swa_prev, p["attn_bias"], j, *attn_tile)
            new_kv.append(kv)
        elif ssm_h0 is None:
            a, cq, states, conv_tail = _ssm_front(xf, p, i, j, big["w_in"][i], b, ssd_tile, states)
            a = a.reshape(b, l, SSM_INNER)
            new_conv.append(conv_tail[:, 8 - (CONV_W - 1):])
        else:
            z, xbc, cq, dt = _norm_matmul(xf, p["norm_mix_pre"], i,
                                          [(big["w_in"][i], SSM_MAIN_SEGS, (F32, F32)),
                                           (p["w_in_side"], SSM_SIDE_SEGS, (BF16, F32))], j, tm)
            xbc = xbc.reshape(b, l, CONV_DIM)
            a, states = _ssd(z.reshape(b, l, SSM_INNER), xbc, dt.reshape(b, l, LANES), conv_hist, ssm_h0, p, j,
                             ssd_tile, states)
            new_conv.append(xbc[:, l - (CONV_W - 1):])
        cast = [raw[k][i + 1] for k in BIG_WEIGHTS] if raw is not None and i + 1 < DEPTH else []
        x, *copies = _tail(a, cq.reshape(b, l, MEM_WIDTH), mem_k, mem_v, x, p, i, big["w_out"][i], big["w_up"][i],
                           big["w_down"][i], *tail_tile, cast)
        for k, w in zip(BIG_WEIGHTS, copies):
            big[k][i + 1] = w
    return x, new_kv, states, jnp.stack(new_conv)


def kernel(x_prompt, x_sample, mem_prompt, cache_swa_k, cache_swa_v, state_ssm, state_conv, cache_mem_k, cache_mem_v,
           norm_mix_pre, norm_mix_post, norm_ffn_pre, norm_ffn_post, w_in_attn, attn_sinks, w_in_ssm, conv_w, conv_b,
           dt_bias, a_log, d_skip, ssm_norm, mem_norm, w_mem_kv, w_out, w_up, w_down):
    p = _prep_weights(norm_mix_pre, norm_mix_post, norm_ffn_pre, norm_ffn_post, attn_sinks, w_in_ssm, conv_w, conv_b,
                      dt_bias, a_log, d_skip, ssm_norm)
    bp, lp, _ = x_prompt.shape
    bs, ls, _ = x_sample.shape
    n_attn = w_in_attn.shape[0]
    w_in_ssm_t = jnp.swapaxes(w_in_ssm, 1, 2)
    raw = dict(w_in=[(w_in_ssm_t, i // 2, w_in_ssm.shape[2]) if i % 2 else (w_in_attn, i // 2, None)
                     for i in range(DEPTH)],
               w_out=[(w_out, i, None) for i in range(DEPTH)], w_up=[(w_up, i, None) for i in range(DEPTH)],
               w_down=[(w_down, i, None) for i in range(DEPTH)])
    big = {k: [raw[k][0][0][0].astype(BF16)] + [None] * (DEPTH - 1) for k in BIG_WEIGHTS}

    mem_rows = N_MEM * MEM_HEADS
    prompt_mem_k, prompt_mem_v = _mem_kv(mem_prompt.reshape(bp * N_MEM, D_MODEL),
                                         mem_norm.astype(F32).reshape(DEPTH, 1, D_MODEL), w_mem_kv.astype(BF16), 512)

    y_prompt, kv_p, prompt_ssm, prompt_conv = _trunk(
        x_prompt, prompt_mem_k.reshape(DEPTH, bp, mem_rows, MEM_HEAD_DIM),
        prompt_mem_v.reshape(DEPTH, bp, mem_rows, MEM_HEAD_DIM), None, None, None, p, big, raw,
        tm=1024, attn_tile=(1, 256), ssd_tile=256, tail_tile=(1, 512))

    keep = cache_swa_k.shape[2]
    swa_prev = jnp.concatenate([cache_swa_k.reshape(n_attn, bs, keep, KV_WIDTH),
                                cache_swa_v.reshape(n_attn, bs, keep, KV_WIDTH)], axis=-1)
    hist = jnp.pad(state_conv, ((0, 0), (0, 0), (8 - (CONV_W - 1), 0), (0, 0)))
    y_sample, kv_s, sample_ssm, sample_conv = _trunk(
        x_sample, cache_mem_k.reshape(DEPTH, bs, mem_rows, MEM_HEAD_DIM),
        cache_mem_v.reshape(DEPTH, bs, mem_rows, MEM_HEAD_DIM), swa_prev, state_ssm, hist, p, big, None,
        tm=512, attn_tile=(4, ls), ssd_tile=ls, tail_tile=(8, ls))

    def split_kv(kvs, rows):
        kv = jnp.stack([kv[:, -rows:] for kv in kvs])
        shape = kv.shape[:3] + (N_KV_HEADS, HEAD_DIM)
        return kv[..., :KV_WIDTH].reshape(shape), kv[..., KV_WIDTH:].reshape(shape)

    prompt_swa_k, prompt_swa_v = split_kv(kv_p, min(WINDOW, lp))
    sample_swa_k, sample_swa_v = split_kv(kv_s, ls)
    mem_shape = (DEPTH, bp, N_MEM, MEM_HEADS, MEM_HEAD_DIM)
    return (y_prompt, y_sample, prompt_swa_k, prompt_swa_v, prompt_ssm, prompt_conv,
            prompt_mem_k.reshape(mem_shape), prompt_mem_v.reshape(mem_shape),
            sample_swa_k, sample_swa_v, sample_ssm, sample_conv)
```

```python
import functools
import math

import numpy as np
import jax
import jax.numpy as jnp
from jax import lax
from jax.experimental import pallas as pl
from jax.experimental.pallas import tpu as pltpu

F32 = jnp.float32
BF16 = jnp.bfloat16

D_MODEL = 1024
DEPTH = 4
CHUNK = 64
N_HEADS = 24
N_KV_HEADS = 3
HEAD_DIM = 64
WINDOW = 128
Q_WIDTH = N_HEADS * HEAD_DIM
KV_WIDTH = N_KV_HEADS * HEAD_DIM
N_MEM = 256
MEM_HEADS = 4
MEM_HEAD_DIM = 128
MEM_WIDTH = MEM_HEADS * MEM_HEAD_DIM
SSM_INNER = 1536
SSM_HEAD_DIM = 64
SSM_HEADS = SSM_INNER // SSM_HEAD_DIM
SSM_GROUPS = 8
SSM_GROUP_WIDTH = SSM_INNER // SSM_GROUPS
D_STATE = 128
CONV_W = 4
CONV_DIM = SSM_INNER + 2 * SSM_GROUPS * D_STATE
D_FF = 4 * D_MODEL
EPS = 1e-6

LANES = 128
HEAD_PAIRS = SSM_HEADS // 2
NEG = -1e30
TRANSPOSE_SLAB = 2 * LANES
VMEM_LIMIT = 56 * 1024 * 1024

NT_DIMS = (((1,), (1,)), ((), ()))


def _rms(x, w):
    return x * lax.rsqrt(jnp.mean(x * x, axis=-1, keepdims=True) + EPS) * w


def _sigmoid(x):
    return 1.0 / (1.0 + jnp.exp(-x))


def _split3(x):
    a = x.astype(BF16)
    r = x - a.astype(F32)
    b = r.astype(BF16)
    c = (r - b.astype(F32)).astype(BF16)
    return a, b, c


def _resident(shape):
    return pl.BlockSpec(shape, lambda *_: (0,) * len(shape), pipeline_mode=pl.Buffered(1))


def _layer(stacked, layer):
    shape = stacked.shape[1:]
    return pl.BlockSpec((None,) + shape, lambda *_: (layer,) + (0,) * len(shape), pipeline_mode=pl.Buffered(1))


def _cast_specs(cast, steps, slab):
    ins, outs, shapes = [], [], []
    for w, lw, cols in cast:
        if cols is None:
            rows = w.shape[1] // steps
            ins.append(pl.BlockSpec((None, rows, w.shape[2]), lambda *g, lw=lw: (lw, slab(*g), 0)))
            outs.append(pl.BlockSpec((rows, w.shape[2]), lambda *g: (slab(*g), 0)))
            shapes.append(jax.ShapeDtypeStruct(w.shape[1:], BF16))
        else:
            last = -(-cols // TRANSPOSE_SLAB) - 1
            ins.append(pl.BlockSpec((None, TRANSPOSE_SLAB, w.shape[2]),
                                    lambda *g, lw=lw, last=last: (lw, jnp.minimum(slab(*g), last), 0)))
            outs.append(pl.BlockSpec((w.shape[2], TRANSPOSE_SLAB),
                                     lambda *g, last=last: (0, jnp.minimum(slab(*g), last))))
            shapes.append(jax.ShapeDtypeStruct((w.shape[2], cols), BF16))
    return ins, outs, shapes


def _cast_slabs(srcs, dsts):
    for src, dst in zip(srcs, dsts):
        w = src[...]
        dst[...] = (w if w.shape == dst.shape else w.T).astype(dst.dtype)


def _norm_matmul_kernel(x_ref, nw_ref, *refs, plan):
    w_refs, o_refs = refs[:len(plan)], list(refs[len(plan):])
    h = _rms(x_ref[...], nw_ref[...]).astype(BF16)
    for w_ref, segs in zip(w_refs, plan):
        for start, width in segs:
            o_ref = o_refs.pop(0)
            for c0 in range(0, width, 512):
                cw = min(512, width - c0)
                o_ref[:, c0:c0 + cw] = jnp.dot(
                    h, w_ref[:, start + c0:start + c0 + cw].astype(BF16), preferred_element_type=F32).astype(o_ref.dtype)


def _norm_matmul(x, nw, nw_layer, weights, w_layer, tm):
    m, k = x.shape
    plan = tuple(segs for _, segs, _ in weights)
    widths = [wd for _, segs, _ in weights for _, wd in segs]
    dtypes = [dt for _, _, dts in weights for dt in dts]
    return pl.pallas_call(
        functools.partial(_norm_matmul_kernel, plan=plan),
        grid=(m // tm,),
        in_specs=([pl.BlockSpec((tm, k), lambda i: (i, 0)), _layer(nw, nw_layer)]
                  + [_layer(w, w_layer) if w.ndim == 3 else _resident(w.shape) for w, _, _ in weights]),
        out_specs=[pl.BlockSpec((tm, wd), lambda i: (i, 0)) for wd in widths],
        out_shape=[jax.ShapeDtypeStruct((m, wd), dt) for wd, dt in zip(widths, dtypes)],
        compiler_params=pltpu.CompilerParams(dimension_semantics=("parallel",), vmem_limit_bytes=VMEM_LIMIT),
        name="norm_matmul",
    )(x, nw, *[w for w, _, _ in weights])


def _mem_kv_kernel(x_ref, nw_ref, w_ref, k_ref, v_ref, wb):
    tm = x_ref.shape[0]

    @pl.when(pl.program_id(1) == 0)
    def _cast():
        wb[...] = w_ref[...].astype(BF16)

    h = _rms(x_ref[...], nw_ref[...]).astype(BF16)
    for o_ref, c0 in ((k_ref, 0), (v_ref, MEM_WIDTH)):
        r = jnp.dot(h, wb[:, c0:c0 + MEM_WIDTH], preferred_element_type=F32)
        for hd in range(MEM_HEADS):
            o_ref[pl.ds(hd, tm, stride=MEM_HEADS), :] = r[:, hd * MEM_HEAD_DIM:(hd + 1) * MEM_HEAD_DIM]


def _mem_kv(mem, nw, w, tm):
    m, k = mem.shape
    out = jax.ShapeDtypeStruct((DEPTH, m * MEM_HEADS, MEM_HEAD_DIM), F32)
    out_spec = pl.BlockSpec((None, tm * MEM_HEADS, MEM_HEAD_DIM), lambda l, i: (l, i, 0))
    return pl.pallas_call(
        _mem_kv_kernel,
        grid=(DEPTH, m // tm),
        in_specs=[pl.BlockSpec((tm, k), lambda l, i: (i, 0)),
                  pl.BlockSpec((None, 1, k), lambda l, i: (l, 0, 0)),
                  pl.BlockSpec((None, k, 2 * MEM_WIDTH), lambda l, i: (l, 0, 0))],
        out_specs=[out_spec, out_spec],
        out_shape=[out, out],
        scratch_shapes=[pltpu.VMEM((k, 2 * MEM_WIDTH), BF16)],
        compiler_params=pltpu.CompilerParams(dimension_semantics=("arbitrary", "arbitrary"),
                                             vmem_limit_bytes=VMEM_LIMIT),
        name="mem_kv",
    )(mem, nw, w)


def _attn_bias_table(sinks):
    slopes = jnp.asarray(2.0 ** (-8.0 * np.arange(1, N_HEADS + 1) / N_HEADS), dtype=F32).reshape(N_KV_HEADS, 4, 1, 2, 1)
    span = WINDOW + CHUNK
    i = np.arange(CHUNK)[:, None]
    s = np.arange(2 * LANES)[None, :]
    dist = jnp.asarray(np.abs(WINDOW + i - s), dtype=F32)[None, None, :, None, :]
    slot = jnp.asarray(s)[None, None, :, None, :]
    bias = -slopes * dist
    bias = jnp.where(slot < span, bias, NEG)
    bias = jnp.where(slot == span, sinks.astype(F32).reshape(N_KV_HEADS, 4, 1, 2, 1), bias)
    return bias.reshape(N_KV_HEADS, 4 * CHUNK, 4 * LANES)


def _attn_kernel(q_ref, kvc_ref, kvp_ref, bias_ref, *refs, nb, T, mask_first):
    n_cast = (len(refs) - 2) // 2
    o_ref, kvbuf = refs[n_cast], refs[-1]
    _cast_slabs(refs[:n_cast], refs[n_cast + 1:-1])
    first = pl.program_id(1) == 0
    lo = lax.broadcasted_iota(jnp.int32, (1, LANES), 1) < HEAD_DIM
    slot = lax.broadcasted_iota(jnp.int32, (1, 4 * LANES), 1) & (2 * LANES - 1)
    zpad = jnp.zeros((CHUNK, LANES), BF16)
    ones_halves = jnp.where(
        (lax.broadcasted_iota(jnp.int32, (4 * LANES, LANES), 0) < 2 * LANES)
        == (lax.broadcasted_iota(jnp.int32, (4 * LANES, LANES), 1) < HEAD_DIM), 1.0, 0.0).astype(BF16)
    span = WINDOW + CHUNK
    for n in range(nb):
        kvbuf[0:WINDOW, :] = kvp_ref[n]
        kvbuf[WINDOW:WINDOW + T, :] = kvc_ref[n]
        for j in range(T // CHUNK):
            win = kvbuf[j * CHUNK:j * CHUNK + span, :]
            blocks = [win[:, m * LANES:(m + 1) * LANES] for m in range(3)]
            rolled = [pltpu.roll(b, HEAD_DIM, 1) for b in blocks]

            def halves(idx):
                m, half = divmod(idx, 2)
                in_lo, in_hi = (blocks[m], rolled[m]) if half == 0 else (rolled[m], blocks[m])
                return jnp.where(lo, in_lo, 0.0).astype(BF16), jnp.where(lo, 0.0, in_hi).astype(BF16)

            for kvh in range(N_KV_HEADS):
                klo, khi = halves(kvh)
                vlo, vhi = halves(N_KV_HEADS + kvh)
                kbd = jnp.concatenate([klo, zpad, khi, zpad], axis=0)
                c0 = kvh * 4 * LANES
                qs = jnp.concatenate(
                    [q_ref[n, j * CHUNK:(j + 1) * CHUNK, c0 + p * LANES:c0 + (p + 1) * LANES] for p in range(4)],
                    axis=0)
                s = lax.dot_general(qs * (1.0 / math.sqrt(HEAD_DIM)), kbd, NT_DIMS, preferred_element_type=F32)
                s = s + bias_ref[kvh]
                if mask_first and j < WINDOW // CHUNK:
                    s = jnp.where(slot < jnp.where(first, WINDOW - j * CHUNK, 0), NEG, s)
                ps = []
                for e in range(2):
                    se = s[:, e * 2 * LANES:(e + 1) * 2 * LANES]
                    ps.append(jnp.exp(se - jnp.max(se, axis=-1, keepdims=True)).astype(BF16))
                od = jnp.dot(jnp.concatenate(ps, axis=1),
                             jnp.concatenate([jnp.concatenate([vlo, zpad, vhi, zpad], axis=0), ones_halves], axis=1),
                             preferred_element_type=F32)
                o = od[:, :LANES] / od[:, LANES:]
                for p in range(4):
                    o_ref[n, j * CHUNK:(j + 1) * CHUNK, c0 + p * LANES:c0 + (p + 1) * LANES] = (
                        o[p * CHUNK:(p + 1) * CHUNK].astype(o_ref.dtype))


def _attention(q, kv, kv_prev, bias, layer, nb, T, cast=()):
    b, l, _ = q.shape
    cast_in, cast_out, cast_shape = _cast_specs(cast, (b // nb) * (l // T), lambda bi, i: bi * (l // T) + i)
    mask_first = kv_prev is None
    if mask_first:
        prev_arr = kv
        prev_spec = pl.BlockSpec((nb, WINDOW, 2 * KV_WIDTH),
                                 lambda bi, i: (bi, jnp.maximum(i * (T // WINDOW) - 1, 0), 0))
    else:
        prev_arr = kv_prev
        prev_spec = pl.BlockSpec((None, nb, WINDOW, 2 * KV_WIDTH), lambda bi, i: (layer, bi, 0, 0))
    return pl.pallas_call(
        functools.partial(_attn_kernel, nb=nb, T=T, mask_first=mask_first),
        grid=(b // nb, l // T),
        in_specs=[pl.BlockSpec((nb, T, Q_WIDTH), lambda bi, i: (bi, i, 0)),
                  pl.BlockSpec((nb, T, 2 * KV_WIDTH), lambda bi, i: (bi, i, 0)),
                  prev_spec,
                  _layer(bias, layer)] + cast_in,
        out_specs=[pl.BlockSpec((nb, T, Q_WIDTH), lambda bi, i: (bi, i, 0))] + cast_out,
        out_shape=[jax.ShapeDtypeStruct((b, l, Q_WIDTH), BF16)] + cast_shape,
        scratch_shapes=[pltpu.VMEM((WINDOW + T, 2 * KV_WIDTH), F32)],
        compiler_params=pltpu.CompilerParams(dimension_semantics=("parallel", "parallel"),
                                             vmem_limit_bytes=VMEM_LIMIT),
        name="swa_attention",
    )(q, kv, prev_arr, bias, *[w for w, _, _ in cast])


def _load_state(st, h0_ref):
    for hp in range(HEAD_PAIRS):
        st[hp] = h0_ref[0, 2 * hp:2 * hp + 2].reshape(2 * SSM_HEAD_DIM, D_STATE).T


def _store_state(hout_ref, st, prev_ref):
    n_prev = hout_ref.shape[0] - 1
    if prev_ref is not None:
        hout_ref[0:n_prev] = prev_ref[...]
    for hp in range(HEAD_PAIRS):
        hout_ref[n_prev, 0, 2 * hp:2 * hp + 2] = st[hp].T.reshape(2, SSM_HEAD_DIM, D_STATE)


def _ssd_kernel(z_ref, xbc_ref, dt_ref, hist_ref, h0_ref, *refs, T, has_prev):
    prm, prev_ref, refs = refs[:7], refs[7] if has_prev else None, refs[7 + has_prev:]
    (y_ref, hout_ref, cbuf), scr = refs[:3], refs[3:]
    i = pl.program_id(1)

    @pl.when(i == 0)
    def _init():
        cbuf[0:8, :] = hist_ref[0]
        _load_state(scr[-1], h0_ref)

    cbuf[8:8 + T, :] = xbc_ref[0]
    _ssd_tile(z_ref.at[0], dt_ref.at[0], cbuf, y_ref.at[0], prm, scr, T)
    cbuf[0:8, :] = cbuf[T:T + 8, :]

    @pl.when(i == pl.num_programs(1) - 1)
    def _fin():
        _store_state(hout_ref, scr[-1], prev_ref)


def _ssd_tile(z_ref, dt_ref, cbuf, y_ref, prm, scr, T, side_work=()):
    for work in side_work:
        work()
    cw_ref, cb_ref, dtb_ref, alog_ref, dsk_ref, nw_ref, e64_ref = prm
    ubuf, ybuf, dtx_s, acs_s, st = scr
    row = lax.broadcasted_iota(jnp.int32, (CHUNK, LANES), 0)
    pos = lax.broadcasted_iota(jnp.int32, (CHUNK, LANES), 1) & (CHUNK - 1)
    diag = pos == row
    causal = pos <= row
    lo = lax.broadcasted_iota(jnp.int32, (1, LANES), 1) < SSM_HEAD_DIM
    tri = (lax.broadcasted_iota(jnp.int32, (CHUNK, CHUNK), 1)
           <= lax.broadcasted_iota(jnp.int32, (CHUNK, CHUNK), 0)).astype(F32).astype(BF16)
    a_neg = -jnp.exp(alog_ref[...])
    b_off = SSM_INNER
    c_off = SSM_INNER + SSM_GROUPS * D_STATE

    for j in range(T // CHUNK):
        r0 = 8 + j * CHUNK
        rows = slice(j * CHUNK, (j + 1) * CHUNK)
        for c0 in range(0, CONV_DIM, LANES):
            cols = slice(c0, c0 + LANES)
            xw = cbuf[r0 - 8:r0 + CHUNK, cols]
            xw1 = pltpu.roll(xw, 1, 0)
            near = xw * cw_ref[3:4, cols] + xw1 * cw_ref[2:3, cols]
            far = pltpu.roll(xw * cw_ref[1:2, cols] + xw1 * cw_ref[0:1, cols], 2, 0)
            acc = (near + far)[8:] + cb_ref[:, cols]
            ubuf[:, cols] = acc * _sigmoid(acc)

        dt_in = dt_ref[rows, :] + dtb_ref[...]
        dtv = jnp.maximum(dt_in, 0.0) + jnp.log1p(jnp.exp(-jnp.abs(dt_in)))
        cum3 = jnp.dot(tri, jnp.concatenate(_split3(dtv * a_neg), axis=1), preferred_element_type=F32)
        cum = cum3[:, 2 * LANES:] + cum3[:, LANES:2 * LANES] + cum3[:, :LANES]
        terms = jnp.concatenate([jnp.concatenate(_split3(dtv), axis=1), jnp.concatenate(_split3(cum), axis=1)], axis=0)
        both = jnp.dot(terms, e64_ref[...], preferred_element_type=F32)
        dtx_s[...] = both[:CHUNK]
        acs_s[...] = both[CHUNK:]

        bb, bt, cb, cbd = [], [], [], []
        for g in range(SSM_GROUPS):
            bg = ubuf[:, b_off + g * D_STATE:b_off + (g + 1) * D_STATE]
            bb.append(bg.astype(BF16))
            bt.append(bg.T.astype(BF16))
            cb.append(ubuf[:, c_off + g * D_STATE:c_off + (g + 1) * D_STATE].astype(BF16))
            cbd.append(lax.dot_general(cb[g], jnp.concatenate([bb[g], bb[g]], axis=0), NT_DIMS,
                                       preferred_element_type=F32))

        for hp in range(HEAD_PAIRS):
            sl = slice(hp * LANES, (hp + 1) * LANES)
            g0, g1 = (2 * hp) // 3, (2 * hp + 1) // 3
            acs = acs_s[:, sl]
            acs_key = jnp.sum(jnp.where(diag, acs, 0.0), axis=0, keepdims=True)
            lmat = jnp.exp(jnp.where(causal, acs - acs_key, NEG))
            eacs = jnp.exp(acs)
            acs_last = acs[CHUNK - 1:CHUNK, :]
            xp = ubuf[:, sl]
            xdt = xp * dtx_s[:, sl]
            xlo = jnp.where(lo, xdt, 0.0)
            xhi = jnp.where(lo, 0.0, xdt)

            cbp = cbd[g0] if g0 == g1 else jnp.where(lo, cbd[g0], cbd[g1])
            xbd = jnp.concatenate([xlo, xhi], axis=0).astype(BF16)
            yd = jnp.dot((cbp * lmat).astype(BF16), xbd, preferred_element_type=F32)

            ps = st[hp]
            psb = ps.astype(BF16)
            yo = jnp.dot(cb[g0], psb, preferred_element_type=F32)
            if g0 != g1:
                yo = jnp.where(lo, yo, jnp.dot(cb[g1], psb, preferred_element_type=F32))
            ybuf[:, sl] = yd + yo * eacs + dsk_ref[:, sl] * xp

            wx = jnp.exp(acs_last - acs)
            if g0 == g1:
                new = jnp.dot(bt[g0], (xdt * wx).astype(BF16), preferred_element_type=F32)
            else:
                new = (jnp.dot(bt[g0], (xlo * wx).astype(BF16), preferred_element_type=F32)
                       + jnp.dot(bt[g1], (xhi * wx).astype(BF16), preferred_element_type=F32))
            st[hp] = ps * eacs[CHUNK - 1:CHUNK, :] + new

        lane = lax.broadcasted_iota(jnp.int32, (1, LANES), 1)
        blocks = []
        for v in range(SSM_INNER // LANES):
            ga, gb = (v * LANES) // SSM_GROUP_WIDTH, (v * LANES + LANES - 1) // SSM_GROUP_WIDTH
            blocks.append((slice(v * LANES, (v + 1) * LANES), ga, gb, lane < gb * SSM_GROUP_WIDTH - v * LANES))
        ss = [None] * SSM_GROUPS
        for vs, ga, gb, in_ga in blocks:
            zv = z_ref[rows, vs]
            yv = ybuf[:, vs] * (zv * _sigmoid(zv))
            ybuf[:, vs] = yv
            y2 = yv * yv
            parts = [(ga, y2)] if ga == gb else [(ga, jnp.where(in_ga, y2, 0.0)), (gb, jnp.where(in_ga, 0.0, y2))]
            for g, part in parts:
                t = jnp.sum(part, axis=-1, keepdims=True)
                ss[g] = t if ss[g] is None else ss[g] + t
        rs = [lax.rsqrt(t / SSM_GROUP_WIDTH + EPS) for t in ss]
        for vs, ga, gb, in_ga in blocks:
            sc = rs[ga] if ga == gb else jnp.where(in_ga, rs[ga], rs[gb])
            y_ref[rows, vs] = (ybuf[:, vs] * sc * nw_ref[:, vs]).astype(y_ref.dtype)


SSM_PARAMS = ("conv_w", "conv_b", "dt_bias", "a_log", "d_skip", "ssm_norm")


def _head_expand():
    one = np.arange(LANES)[:, None] == (np.arange(SSM_INNER)[None, :] // SSM_HEAD_DIM)
    return jnp.asarray(np.tile(one, (3, 1)), dtype=BF16)


def _ssd_scratch():
    return [pltpu.VMEM((CHUNK, CONV_DIM), F32), pltpu.VMEM((CHUNK, SSM_INNER), F32),
            pltpu.VMEM((CHUNK, SSM_INNER), F32), pltpu.VMEM((CHUNK, SSM_INNER), F32),
            pltpu.VMEM((HEAD_PAIRS, D_STATE, LANES), F32)]


def _state_specs(prev, layer, b, seq):
    shape = (SSM_HEADS, SSM_HEAD_DIM, D_STATE)
    spec = lambda n: pl.BlockSpec((n, 1) + shape, lambda *g: (0, seq(*g), 0, 0, 0))
    ins = [] if prev is None else [spec(layer)]
    return ins, spec(layer + 1), jax.ShapeDtypeStruct((layer + 1, b) + shape, F32)


def _ssd(z, xbc, dt, hist, h0, p, layer, T, prev):
    b, l, _ = z.shape
    tok = lambda w: pl.BlockSpec((1, T, w), lambda bi, i: (bi, i, 0))
    prev_specs, state_spec, state_shape = _state_specs(prev, layer, b, lambda bi, i: bi)
    return pl.pallas_call(
        functools.partial(_ssd_kernel, T=T, has_prev=len(prev_specs)),
        grid=(b, l // T),
        in_specs=[tok(SSM_INNER), tok(CONV_DIM), tok(LANES),
                  pl.BlockSpec((None, 1, 8, CONV_DIM), lambda bi, i: (layer, bi, 0, 0)),
                  pl.BlockSpec((None, 1, SSM_HEADS, SSM_HEAD_DIM, D_STATE), lambda bi, i: (layer, bi, 0, 0, 0)),
                  *[_layer(p[k], layer) for k in SSM_PARAMS],
                  _resident((3 * LANES, SSM_INNER))] + prev_specs,
        out_specs=[tok(SSM_INNER), state_spec],
        out_shape=[jax.ShapeDtypeStruct((b, l, SSM_INNER), BF16), state_shape],
        scratch_shapes=[pltpu.VMEM((8 + T, CONV_DIM), F32)] + _ssd_scratch(),
        compiler_params=pltpu.CompilerParams(dimension_semantics=("parallel", "arbitrary"),
                                             vmem_limit_bytes=VMEM_LIMIT),
        name="conv_ssd",
    )(z, xbc, dt, hist, h0, *[p[k] for k in SSM_PARAMS], _head_expand(), *([] if prev is None else [prev]))


def _ssm_front_kernel(x_ref, nw_ref, wmain_ref, wside_ref, *refs, T, nt, has_prev):
    prm, prev_ref, refs = refs[:7], refs[7] if has_prev else None, refs[7 + has_prev:]
    (y_ref, cq_ref, hout_ref, tail_ref), scr, (zbuf, dtbuf, cbuf) = refs[:4], refs[4:-3], refs[-3:]
    s = pl.program_id(0)
    slot = lax.rem(s, 2)
    prev = 1 - slot
    i = lax.rem(s + nt - 1, nt)

    @pl.when(s == 0)
    def _first():
        zbuf[1] = jnp.zeros((T, SSM_INNER), F32)
        dtbuf[1] = jnp.zeros((T, LANES), F32)
        cbuf[1] = jnp.zeros((8 + T, CONV_DIM), F32)

    @pl.when((s == 0) | (i == 0))
    def _init():
        cbuf[prev, 0:8, :] = jnp.zeros((8, CONV_DIM), F32)
        for hp in range(HEAD_PAIRS):
            scr[-1][hp] = jnp.zeros((D_STATE, LANES), F32)

    h = _rms(x_ref[...], nw_ref[...]).astype(BF16)

    def project(w_ref, c0, width, store):
        return lambda: store(jnp.dot(h, w_ref[:, c0:c0 + width].astype(BF16), preferred_element_type=F32))

    def z_store(c0):
        def store(v):
            zbuf[slot, :, c0:c0 + 512] = v
        return store

    def xbc_store(c0):
        def store(v):
            cbuf[slot, 8:8 + T, c0:c0 + 512] = v
        return store

    def cq_store(v):
        cq_ref[...] = v.astype(cq_ref.dtype)

    def dt_store(v):
        dtbuf[slot] = v

    projections = ([project(wmain_ref, c0, 512, z_store(c0)) for c0 in range(0, SSM_INNER, 512)]
                   + [project(wmain_ref, SSM_INNER + c0, 512, xbc_store(c0)) for c0 in range(0, CONV_DIM, 512)]
                   + [project(wside_ref, 0, MEM_WIDTH, cq_store), project(wside_ref, MEM_WIDTH, LANES, dt_store)])
    _ssd_tile(zbuf.at[prev], dtbuf.at[prev], cbuf.at[prev], y_ref, prm, scr, T, projections)
    cbuf[slot, 0:8, :] = cbuf[prev, T:T + 8, :]

    @pl.when((s > 0) & (i == nt - 1))
    def _fin():
        _store_state(hout_ref, scr[-1], prev_ref)
        tail_ref[0] = cbuf[prev, T:T + 8, :]


def _ssm_front(x, p, i_layer, layer, w_main, b, T, prev):
    m = x.shape[0]
    nt = m // (b * T)
    n = b * nt
    seq = lambda s: jnp.maximum(s - 1, 0) // nt
    prev_specs, state_spec, state_shape = _state_specs(prev, layer, b, seq)
    return pl.pallas_call(
        functools.partial(_ssm_front_kernel, T=T, nt=nt, has_prev=len(prev_specs)),
        grid=(n + 1,),
        in_specs=[pl.BlockSpec((T, D_MODEL), lambda s: (jnp.minimum(s, n - 1), 0)),
                  _layer(p["norm_mix_pre"], i_layer), _resident(w_main.shape), _layer(p["w_in_side"], layer),
                  *[_layer(p[k], layer) for k in SSM_PARAMS],
                  _resident((3 * LANES, SSM_INNER))] + prev_specs,
        out_specs=[pl.BlockSpec((T, SSM_INNER), lambda s: (jnp.maximum(s - 1, 0), 0)),
                   pl.BlockSpec((T, MEM_WIDTH), lambda s: (jnp.minimum(s, n - 1), 0)),
                   state_spec,
                   pl.BlockSpec((1, 8, CONV_DIM), lambda s: (seq(s), 0, 0))],
        out_shape=[jax.ShapeDtypeStruct((m, SSM_INNER), BF16), jax.ShapeDtypeStruct((m, MEM_WIDTH), BF16),
                   state_shape, jax.ShapeDtypeStruct((b, 8, CONV_DIM), F32)],
        scratch_shapes=_ssd_scratch() + [pltpu.VMEM((2, T, SSM_INNER), F32), pltpu.VMEM((2, T, LANES), F32),
                                         pltpu.VMEM((2, 8 + T, CONV_DIM), F32)],
        compiler_params=pltpu.CompilerParams(dimension_semantics=("arbitrary",), vmem_limit_bytes=VMEM_LIMIT),
        name="ssm_front",
    )(x, p["norm_mix_pre"], w_main, p["w_in_side"], *[p[k] for k in SSM_PARAMS], _head_expand(),
      *([] if prev is None else [prev]))


def _tail_kernel(a_ref, cq_ref, mk_ref, mv_ref, x_ref, wout_ref, npost_ref, nfpre_ref, wup_ref, wdn_ref, nfpost_ref,
                 *refs, nb, T):
    n_cast = len(refs) // 2
    o_ref = refs[n_cast]
    _cast_slabs(refs[:n_cast], refs[n_cast + 1:])
    m = nb * T
    cs = []
    for n in range(nb):
        outs = []
        for h in range(MEM_HEADS):
            hs = slice(h * MEM_HEAD_DIM, (h + 1) * MEM_HEAD_DIM)
            head_rows = pl.ds(h, N_MEM, stride=MEM_HEADS)
            mk = mk_ref[n, head_rows, :].astype(BF16)
            mv = mv_ref[n, head_rows, :].astype(BF16)
            s = lax.dot_general(cq_ref[n, :, hs], mk, NT_DIMS, preferred_element_type=F32)
            s = s * (1.0 / math.sqrt(MEM_HEAD_DIM))
            p = jnp.exp(s - jnp.max(s, axis=-1, keepdims=True))
            den = jnp.sum(p, axis=-1, keepdims=True)
            o = jnp.dot(p.astype(BF16), mv, preferred_element_type=F32) * (1.0 / den)
            outs.append(o.astype(BF16))
        cs.append(jnp.concatenate(outs, axis=1))
    c = cs[0] if nb == 1 else jnp.concatenate(cs, axis=0)
    a = a_ref[...].reshape(m, Q_WIDTH)
    mix = (jnp.dot(a, wout_ref[0:Q_WIDTH, :], preferred_element_type=F32)
           + jnp.dot(c, wout_ref[Q_WIDTH:, :], preferred_element_type=F32))
    x1 = x_ref[...].reshape(m, D_MODEL) + _rms(mix, npost_ref[...])
    h2 = _rms(x1, nfpre_ref[...]).astype(BF16)
    acc = None
    for f0 in range(0, D_FF, 1024):
        u = jnp.maximum(jnp.dot(h2, wup_ref[:, f0:f0 + 1024], preferred_element_type=F32), 0.0)
        d = jnp.dot((u * u).astype(BF16), wdn_ref[f0:f0 + 1024, :], preferred_element_type=F32)
        acc = d if acc is None else acc + d
    o_ref[...] = (x1 + _rms(acc, nfpost_ref[...])).reshape(nb, T, D_MODEL)


def _tail(a, cq, mk, mv, x, p, layer, w_out, w_up, w_down, nb, T, cast=()):
    b, l, _ = x.shape
    steps = (b // nb) * (l // T)
    tok = lambda w: pl.BlockSpec((nb, T, w), lambda bi, i: (bi, i, 0))
    mem = pl.BlockSpec((None, nb, N_MEM * MEM_HEADS, MEM_HEAD_DIM), lambda bi, i: (layer, bi, 0, 0))
    norm = lambda k: _layer(p[k], layer)
    cast_in, cast_out, cast_shape = _cast_specs(cast, steps, lambda bi, i: bi * (l // T) + i)
    return pl.pallas_call(
        functools.partial(_tail_kernel, nb=nb, T=T),
        grid=(b // nb, l // T),
        in_specs=[tok(Q_WIDTH), tok(MEM_WIDTH), mem, mem, tok(D_MODEL),
                  _resident(w_out.shape), norm("norm_mix_post"), norm("norm_ffn_pre"), _resident(w_up.shape),
                  _resident(w_down.shape), norm("norm_ffn_post")] + cast_in,
        out_specs=[tok(D_MODEL)] + cast_out,
        out_shape=[jax.ShapeDtypeStruct((b, l, D_MODEL), F32)] + cast_shape,
        compiler_params=pltpu.CompilerParams(dimension_semantics=("arbitrary", "arbitrary"),
                                             vmem_limit_bytes=VMEM_LIMIT),
        name="xattn_outproj_mlp",
    )(a, cq, mk, mv, x, w_out, p["norm_mix_post"], p["norm_ffn_pre"], w_up, w_down, p["norm_ffn_post"],
      *[w for w, _, _ in cast])


ATTN_SEGS = ((0, Q_WIDTH), (Q_WIDTH, 2 * KV_WIDTH), (Q_WIDTH + 2 * KV_WIDTH, MEM_WIDTH))
SSM_MAIN_SEGS = ((0, SSM_INNER), (SSM_INNER, CONV_DIM))
SSM_SIDE_SEGS = ((0, MEM_WIDTH), (MEM_WIDTH, LANES))


def _prep_weights(norm_mix_pre, norm_mix_post, norm_ffn_pre, norm_ffn_post, attn_sinks, w_in_ssm, conv_w,
                  conv_b, dt_bias, a_log, d_skip, ssm_norm):
    dt_lo = SSM_INNER + CONV_DIM
    dt_hi = dt_lo + SSM_HEADS
    pad = jnp.zeros((w_in_ssm.shape[0], D_MODEL, LANES - SSM_HEADS), F32)
    w_side = jnp.concatenate([w_in_ssm[..., dt_hi:], w_in_ssm[..., dt_lo:dt_hi], pad], axis=-1)
    vec = lambda v: v.astype(F32).reshape(v.shape[0], 1, -1)
    lane_pad = lambda v: vec(jnp.pad(v, ((0, 0), (0, LANES - SSM_HEADS))))
    return dict(
        norm_mix_pre=vec(norm_mix_pre), norm_mix_post=vec(norm_mix_post), norm_ffn_pre=vec(norm_ffn_pre),
        norm_ffn_post=vec(norm_ffn_post), w_in_side=w_side,
        attn_bias=jnp.stack([_attn_bias_table(attn_sinks[j]) for j in range(attn_sinks.shape[0])]),
        conv_w=conv_w.astype(F32), conv_b=vec(conv_b), dt_bias=lane_pad(dt_bias), a_log=lane_pad(a_log),
        d_skip=vec(jnp.repeat(d_skip, SSM_HEAD_DIM, axis=-1)), ssm_norm=vec(ssm_norm))


BIG_WEIGHTS = ("w_in", "w_out", "w_up", "w_down")


def _trunk(x, mem_k, mem_v, swa_prev, ssm_h0, conv_hist, p, big, raw, *, tm, attn_tile, ssd_tile, tail_tile):
    b, l, _ = x.shape
    m = b * l
    new_kv, new_conv, states = [], [], None
    for i in range(DEPTH):
        j = i // 2
        xf = x.reshape(m, D_MODEL)
        if i % 2 == 0:
            q, kv, cq = _norm_matmul(xf, p["norm_mix_pre"], i, [(big["w_in"][i], ATTN_SEGS, (BF16, F32, BF16))], j, tm)
            kv = kv.reshape(b, l, 2 * KV_WIDTH)
            missing = [k for k in BIG_WEIGHTS if big[k][i] is None]
            a, *copies = _attention(q.reshape(b, l, Q_WIDTH), kv, swa_prev, p["attn_bias"], j, *attn_tile,
                                    [raw[k][i] for k in missing])
            for k, w in zip(missing, copies):
                big[k][i] = w
            new_kv.append(kv)
        elif ssm_h0 is None:
            a, cq, states, conv_tail = _ssm_front(xf, p, i, j, big["w_in"][i], b, ssd_tile, states)
            a = a.reshape(b, l, SSM_INNER)
            new_conv.append(conv_tail[:, 8 - (CONV_W - 1):])
        else:
            z, xbc, cq, dt = _norm_matmul(xf, p["norm_mix_pre"], i,
                                          [(big["w_in"][i], SSM_MAIN_SEGS, (F32, F32)),
                                           (p["w_in_side"], SSM_SIDE_SEGS, (BF16, F32))], j, tm)
            xbc = xbc.reshape(b, l, CONV_DIM)
            a, states = _ssd(z.reshape(b, l, SSM_INNER), xbc, dt.reshape(b, l, LANES), conv_hist, ssm_h0, p, j,
                             ssd_tile, states)
            new_conv.append(xbc[:, l - (CONV_W - 1):])
        cast = [raw[k][i + 1] for k in BIG_WEIGHTS] if raw is not None and i + 1 < DEPTH else []
        x, *copies = _tail(a, cq.reshape(b, l, MEM_WIDTH), mem_k, mem_v, x, p, i, big["w_out"][i], big["w_up"][i],
                           big["w_down"][i], *tail_tile, cast)
        for k, w in zip(BIG_WEIGHTS, copies):
            big[k][i + 1] = w
    return x, new_kv, states, jnp.stack(new_conv)


def kernel(x_prompt, x_sample, mem_prompt, cache_swa_k, cache_swa_v, state_ssm, state_conv, cache_mem_k, cache_mem_v,
           norm_mix_pre, norm_mix_post, norm_ffn_pre, norm_ffn_post, w_in_attn, attn_sinks, w_in_ssm, conv_w, conv_b,
           dt_bias, a_log, d_skip, ssm_norm, mem_norm, w_mem_kv, w_out, w_up, w_down):
    p = _prep_weights(norm_mix_pre, norm_mix_post, norm_ffn_pre, norm_ffn_post, attn_sinks, w_in_ssm, conv_w, conv_b,
                      dt_bias, a_log, d_skip, ssm_norm)
    bp, lp, _ = x_prompt.shape
    bs, ls, _ = x_sample.shape
    n_attn = w_in_attn.shape[0]
    w_in_ssm_t = jnp.swapaxes(w_in_ssm, 1, 2)
    raw = dict(w_in=[(w_in_ssm_t, i // 2, w_in_ssm.shape[2]) if i % 2 else (w_in_attn, i // 2, None)
                     for i in range(DEPTH)],
               w_out=[(w_out, i, None) for i in range(DEPTH)], w_up=[(w_up, i, None) for i in range(DEPTH)],
               w_down=[(w_down, i, None) for i in range(DEPTH)])
    big = {k: [None] * DEPTH for k in BIG_WEIGHTS}
    big["w_in"][0] = w_in_attn[0].astype(BF16)

    mem_rows = N_MEM * MEM_HEADS
    prompt_mem_k, prompt_mem_v = _mem_kv(mem_prompt.reshape(bp * N_MEM, D_MODEL),
                                         mem_norm.astype(F32).reshape(DEPTH, 1, D_MODEL), w_mem_kv, 512)

    y_prompt, kv_p, prompt_ssm, prompt_conv = _trunk(
        x_prompt, prompt_mem_k.reshape(DEPTH, bp, mem_rows, MEM_HEAD_DIM),
        prompt_mem_v.reshape(DEPTH, bp, mem_rows, MEM_HEAD_DIM), None, None, None, p, big, raw,
        tm=1024, attn_tile=(1, 256), ssd_tile=256, tail_tile=(1, 512))

    keep = cache_swa_k.shape[2]
    swa_prev = jnp.concatenate([cache_swa_k.reshape(n_attn, bs, keep, KV_WIDTH),
                                cache_swa_v.reshape(n_attn, bs, keep, KV_WIDTH)], axis=-1)
    hist = jnp.pad(state_conv, ((0, 0), (0, 0), (8 - (CONV_W - 1), 0), (0, 0)))
    y_sample, kv_s, sample_ssm, sample_conv = _trunk(
        x_sample, cache_mem_k.reshape(DEPTH, bs, mem_rows, MEM_HEAD_DIM),
        cache_mem_v.reshape(DEPTH, bs, mem_rows, MEM_HEAD_DIM), swa_prev, state_ssm, hist, p, big, None,
        tm=512, attn_tile=(4, ls), ssd_tile=ls, tail_tile=(8, ls))

    def split_kv(kvs, rows):
        kv = jnp.stack([kv[:, -rows:] for kv in kvs])
        shape = kv.shape[:3] + (N_KV_HEADS, HEAD_DIM)
        return kv[..., :KV_WIDTH].reshape(shape), kv[..., KV_WIDTH:].reshape(shape)

    prompt_swa_k, prompt_swa_v = split_kv(kv_p, min(WINDOW, lp))
    sample_swa_k, sample_swa_v = split_kv(kv_s, ls)
    mem_shape = (DEPTH, bp, N_MEM, MEM_HEADS, MEM_HEAD_DIM)
    return (y_prompt, y_sample, prompt_swa_k, prompt_swa_v, prompt_ssm, prompt_conv,
            prompt_mem_k.reshape(mem_shape), prompt_mem_v.reshape(mem_shape),
            sample_swa_k, sample_swa_v, sample_ssm, sample_conv)
```

```python
import functools
import math

import numpy as np
import jax
import jax.numpy as jnp
from jax import lax
from jax.experimental import pallas as pl
from jax.experimental.pallas import tpu as pltpu

F32 = jnp.float32
BF16 = jnp.bfloat16

D_MODEL = 1024
DEPTH = 4
CHUNK = 64
N_HEADS = 24
N_KV_HEADS = 3
HEAD_DIM = 64
WINDOW = 128
Q_WIDTH = N_HEADS * HEAD_DIM
KV_WIDTH = N_KV_HEADS * HEAD_DIM
N_MEM = 256
MEM_HEADS = 4
MEM_HEAD_DIM = 128
MEM_WIDTH = MEM_HEADS * MEM_HEAD_DIM
SSM_INNER = 1536
SSM_HEAD_DIM = 64
SSM_HEADS = SSM_INNER // SSM_HEAD_DIM
SSM_GROUPS = 8
SSM_GROUP_WIDTH = SSM_INNER // SSM_GROUPS
D_STATE = 128
CONV_W = 4
CONV_DIM = SSM_INNER + 2 * SSM_GROUPS * D_STATE
D_FF = 4 * D_MODEL
EPS = 1e-6

LANES = 128
HEAD_PAIRS = SSM_HEADS // 2
NEG = -1e30
TRANSPOSE_SLAB = 2 * LANES
FF_CHUNK = 1024
VMEM_LIMIT = 56 * 1024 * 1024

NT_DIMS = (((1,), (1,)), ((), ()))


def _rms(x, w):
    return x * lax.rsqrt(jnp.mean(x * x, axis=-1, keepdims=True) + EPS) * w


def _sigmoid(x):
    return 1.0 / (1.0 + jnp.exp(-x))


def _split3(x):
    a = x.astype(BF16)
    r = x - a.astype(F32)
    b = r.astype(BF16)
    c = (r - b.astype(F32)).astype(BF16)
    return a, b, c


def _resident(shape):
    return pl.BlockSpec(shape, lambda *_: (0,) * len(shape), pipeline_mode=pl.Buffered(1))


def _layer(stacked, layer):
    shape = stacked.shape[1:]
    return pl.BlockSpec((None,) + shape, lambda *_: (layer,) + (0,) * len(shape), pipeline_mode=pl.Buffered(1))


def _cast_specs(cast, steps, slab):
    ins, outs, shapes = [], [], []
    for w, lw, cols in cast:
        if cols is None:
            rows = w.shape[1] // steps
            ins.append(pl.BlockSpec((None, rows, w.shape[2]), lambda *g, lw=lw: (lw, slab(*g), 0)))
            outs.append(pl.BlockSpec((rows, w.shape[2]), lambda *g: (slab(*g), 0)))
            shapes.append(jax.ShapeDtypeStruct(w.shape[1:], BF16))
        else:
            last = -(-cols // TRANSPOSE_SLAB) - 1
            ins.append(pl.BlockSpec((None, TRANSPOSE_SLAB, w.shape[2]),
                                    lambda *g, lw=lw, last=last: (lw, jnp.minimum(slab(*g), last), 0)))
            outs.append(pl.BlockSpec((w.shape[2], TRANSPOSE_SLAB),
                                     lambda *g, last=last: (0, jnp.minimum(slab(*g), last))))
            shapes.append(jax.ShapeDtypeStruct((w.shape[2], cols), BF16))
    return ins, outs, shapes


def _cast_slabs(srcs, dsts):
    for src, dst in zip(srcs, dsts):
        w = src[...]
        dst[...] = (w if w.shape == dst.shape else w.T).astype(dst.dtype)


def _norm_matmul_kernel(x_ref, nw_ref, *refs, plan):
    w_refs, o_refs = refs[:len(plan)], list(refs[len(plan):])
    h = _rms(x_ref[...], nw_ref[...]).astype(BF16)
    for w_ref, segs in zip(w_refs, plan):
        for start, width in segs:
            o_ref = o_refs.pop(0)
            for c0 in range(0, width, 512):
                cw = min(512, width - c0)
                o_ref[:, c0:c0 + cw] = jnp.dot(
                    h, w_ref[:, start + c0:start + c0 + cw].astype(BF16), preferred_element_type=F32).astype(o_ref.dtype)


def _norm_matmul(x, nw, nw_layer, weights, w_layer, tm):
    m, k = x.shape
    plan = tuple(segs for _, segs, _ in weights)
    widths = [wd for _, segs, _ in weights for _, wd in segs]
    dtypes = [dt for _, _, dts in weights for dt in dts]
    return pl.pallas_call(
        functools.partial(_norm_matmul_kernel, plan=plan),
        grid=(m // tm,),
        in_specs=([pl.BlockSpec((tm, k), lambda i: (i, 0)), _layer(nw, nw_layer)]
                  + [_layer(w, w_layer) if w.ndim == 3 else _resident(w.shape) for w, _, _ in weights]),
        out_specs=[pl.BlockSpec((tm, wd), lambda i: (i, 0)) for wd in widths],
        out_shape=[jax.ShapeDtypeStruct((m, wd), dt) for wd, dt in zip(widths, dtypes)],
        compiler_params=pltpu.CompilerParams(dimension_semantics=("parallel",), vmem_limit_bytes=VMEM_LIMIT),
        name="norm_matmul",
    )(x, nw, *[w for w, _, _ in weights])


def _mem_kv_kernel(x_ref, nw_ref, w_ref, k_ref, v_ref, wb):
    tm = x_ref.shape[0]

    @pl.when(pl.program_id(1) == 0)
    def _cast():
        wb[...] = w_ref[...].astype(BF16)

    h = _rms(x_ref[...], nw_ref[...]).astype(BF16)
    for o_ref, c0 in ((k_ref, 0), (v_ref, MEM_WIDTH)):
        r = jnp.dot(h, wb[:, c0:c0 + MEM_WIDTH], preferred_element_type=F32)
        for hd in range(MEM_HEADS):
            o_ref[pl.ds(hd, tm, stride=MEM_HEADS), :] = r[:, hd * MEM_HEAD_DIM:(hd + 1) * MEM_HEAD_DIM]


def _mem_kv(mem, nw, w, tm):
    m, k = mem.shape
    out = jax.ShapeDtypeStruct((DEPTH, m * MEM_HEADS, MEM_HEAD_DIM), F32)
    out_spec = pl.BlockSpec((None, tm * MEM_HEADS, MEM_HEAD_DIM), lambda l, i: (l, i, 0))
    return pl.pallas_call(
        _mem_kv_kernel,
        grid=(DEPTH, m // tm),
        in_specs=[pl.BlockSpec((tm, k), lambda l, i: (i, 0)),
                  pl.BlockSpec((None, 1, k), lambda l, i: (l, 0, 0)),
                  pl.BlockSpec((None, k, 2 * MEM_WIDTH), lambda l, i: (l, 0, 0))],
        out_specs=[out_spec, out_spec],
        out_shape=[out, out],
        scratch_shapes=[pltpu.VMEM((k, 2 * MEM_WIDTH), BF16)],
        compiler_params=pltpu.CompilerParams(dimension_semantics=("arbitrary", "arbitrary"),
                                             vmem_limit_bytes=VMEM_LIMIT),
        name="mem_kv",
    )(mem, nw, w)


def _attn_bias_table(sinks):
    slopes = jnp.asarray(2.0 ** (-8.0 * np.arange(1, N_HEADS + 1) / N_HEADS), dtype=F32).reshape(N_KV_HEADS, 4, 1, 2, 1)
    span = WINDOW + CHUNK
    i = np.arange(CHUNK)[:, None]
    s = np.arange(2 * LANES)[None, :]
    dist = jnp.asarray(np.abs(WINDOW + i - s), dtype=F32)[None, None, :, None, :]
    slot = jnp.asarray(s)[None, None, :, None, :]
    bias = -slopes * dist
    bias = jnp.where(slot < span, bias, NEG)
    bias = jnp.where(slot == span, sinks.astype(F32).reshape(N_KV_HEADS, 4, 1, 2, 1), bias)
    return bias.reshape(N_KV_HEADS, 4 * CHUNK, 4 * LANES)


def _attn_kernel(q_ref, kvc_ref, kvp_ref, bias_ref, *refs, nb, T, mask_first):
    n_cast = (len(refs) - 2) // 2
    o_ref, kvbuf = refs[n_cast], refs[-1]
    _cast_slabs(refs[:n_cast], refs[n_cast + 1:-1])
    first = pl.program_id(1) == 0
    lo = lax.broadcasted_iota(jnp.int32, (1, LANES), 1) < HEAD_DIM
    slot = lax.broadcasted_iota(jnp.int32, (1, 4 * LANES), 1) & (2 * LANES - 1)
    zpad = jnp.zeros((CHUNK, LANES), BF16)
    ones_halves = jnp.where(
        (lax.broadcasted_iota(jnp.int32, (4 * LANES, LANES), 0) < 2 * LANES)
        == (lax.broadcasted_iota(jnp.int32, (4 * LANES, LANES), 1) < HEAD_DIM), 1.0, 0.0).astype(BF16)
    span = WINDOW + CHUNK
    for n in range(nb):
        kvbuf[0:WINDOW, :] = kvp_ref[n]
        kvbuf[WINDOW:WINDOW + T, :] = kvc_ref[n]
        for j in range(T // CHUNK):
            win = kvbuf[j * CHUNK:j * CHUNK + span, :]
            blocks = [win[:, m * LANES:(m + 1) * LANES] for m in range(3)]
            rolled = [pltpu.roll(b, HEAD_DIM, 1) for b in blocks]

            def halves(idx):
                m, half = divmod(idx, 2)
                in_lo, in_hi = (blocks[m], rolled[m]) if half == 0 else (rolled[m], blocks[m])
                return jnp.where(lo, in_lo, 0.0).astype(BF16), jnp.where(lo, 0.0, in_hi).astype(BF16)

            for kvh in range(N_KV_HEADS):
                klo, khi = halves(kvh)
                vlo, vhi = halves(N_KV_HEADS + kvh)
                kbd = jnp.concatenate([klo, zpad, khi, zpad], axis=0)
                c0 = kvh * 4 * LANES
                qs = jnp.concatenate(
                    [q_ref[n, j * CHUNK:(j + 1) * CHUNK, c0 + p * LANES:c0 + (p + 1) * LANES] for p in range(4)],
                    axis=0)
                s = lax.dot_general(qs * (1.0 / math.sqrt(HEAD_DIM)), kbd, NT_DIMS, preferred_element_type=F32)
                s = s + bias_ref[kvh]
                if mask_first and j < WINDOW // CHUNK:
                    s = jnp.where(slot < jnp.where(first, WINDOW - j * CHUNK, 0), NEG, s)
                ps = []
                for e in range(2):
                    se = s[:, e * 2 * LANES:(e + 1) * 2 * LANES]
                    ps.append(jnp.exp(se - jnp.max(se, axis=-1, keepdims=True)).astype(BF16))
                od = jnp.dot(jnp.concatenate(ps, axis=1),
                             jnp.concatenate([jnp.concatenate([vlo, zpad, vhi, zpad], axis=0), ones_halves], axis=1),
                             preferred_element_type=F32)
                o = od[:, :LANES] / od[:, LANES:]
                for p in range(4):
                    o_ref[n, j * CHUNK:(j + 1) * CHUNK, c0 + p * LANES:c0 + (p + 1) * LANES] = (
                        o[p * CHUNK:(p + 1) * CHUNK].astype(o_ref.dtype))


def _attention(q, kv, kv_prev, bias, layer, nb, T, cast=()):
    b, l, _ = q.shape
    cast_in, cast_out, cast_shape = _cast_specs(cast, (b // nb) * (l // T), lambda bi, i: bi * (l // T) + i)
    mask_first = kv_prev is None
    if mask_first:
        prev_arr = kv
        prev_spec = pl.BlockSpec((nb, WINDOW, 2 * KV_WIDTH),
                                 lambda bi, i: (bi, jnp.maximum(i * (T // WINDOW) - 1, 0), 0))
    else:
        prev_arr = kv_prev
        prev_spec = pl.BlockSpec((None, nb, WINDOW, 2 * KV_WIDTH), lambda bi, i: (layer, bi, 0, 0))
    return pl.pallas_call(
        functools.partial(_attn_kernel, nb=nb, T=T, mask_first=mask_first),
        grid=(b // nb, l // T),
        in_specs=[pl.BlockSpec((nb, T, Q_WIDTH), lambda bi, i: (bi, i, 0)),
                  pl.BlockSpec((nb, T, 2 * KV_WIDTH), lambda bi, i: (bi, i, 0)),
                  prev_spec,
                  _layer(bias, layer)] + cast_in,
        out_specs=[pl.BlockSpec((nb, T, Q_WIDTH), lambda bi, i: (bi, i, 0))] + cast_out,
        out_shape=[jax.ShapeDtypeStruct((b, l, Q_WIDTH), BF16)] + cast_shape,
        scratch_shapes=[pltpu.VMEM((WINDOW + T, 2 * KV_WIDTH), F32)],
        compiler_params=pltpu.CompilerParams(dimension_semantics=("parallel", "parallel"),
                                             vmem_limit_bytes=VMEM_LIMIT),
        name="swa_attention",
    )(q, kv, prev_arr, bias, *[w for w, _, _ in cast])


def _load_state(st, h0_ref):
    for hp in range(HEAD_PAIRS):
        st[hp] = h0_ref[0, 2 * hp:2 * hp + 2].reshape(2 * SSM_HEAD_DIM, D_STATE).T


def _store_state(hout_ref, st, prev_ref):
    n_prev = hout_ref.shape[0] - 1
    if prev_ref is not None:
        hout_ref[0:n_prev] = prev_ref[...]
    for hp in range(HEAD_PAIRS):
        hout_ref[n_prev, 0, 2 * hp:2 * hp + 2] = st[hp].T.reshape(2, SSM_HEAD_DIM, D_STATE)


def _ssd_kernel(z_ref, xbc_ref, dt_ref, hist_ref, h0_ref, *refs, T, has_prev):
    prm, prev_ref, refs = refs[:7], refs[7] if has_prev else None, refs[7 + has_prev:]
    (y_ref, hout_ref, cbuf), scr = refs[:3], refs[3:]
    i = pl.program_id(1)

    @pl.when(i == 0)
    def _init():
        cbuf[0:8, :] = hist_ref[0]
        _load_state(scr[-1], h0_ref)

    cbuf[8:8 + T, :] = xbc_ref[0]
    _ssd_tile(z_ref.at[0], dt_ref.at[0], cbuf, y_ref.at[0], prm, scr, T)
    cbuf[0:8, :] = cbuf[T:T + 8, :]

    @pl.when(i == pl.num_programs(1) - 1)
    def _fin():
        _store_state(hout_ref, scr[-1], prev_ref)


def _ssd_tile(z_ref, dt_ref, cbuf, y_ref, prm, scr, T, side_work=()):
    for work in side_work:
        work()
    cw_ref, cb_ref, dtb_ref, alog_ref, dsk_ref, nw_ref, e64_ref = prm
    ubuf, ybuf, dtx_s, acs_s, st = scr
    row = lax.broadcasted_iota(jnp.int32, (CHUNK, LANES), 0)
    pos = lax.broadcasted_iota(jnp.int32, (CHUNK, LANES), 1) & (CHUNK - 1)
    diag = pos == row
    causal = pos <= row
    lo = lax.broadcasted_iota(jnp.int32, (1, LANES), 1) < SSM_HEAD_DIM
    tri = (lax.broadcasted_iota(jnp.int32, (CHUNK, CHUNK), 1)
           <= lax.broadcasted_iota(jnp.int32, (CHUNK, CHUNK), 0)).astype(F32).astype(BF16)
    a_neg = -jnp.exp(alog_ref[...])
    b_off = SSM_INNER
    c_off = SSM_INNER + SSM_GROUPS * D_STATE

    for j in range(T // CHUNK):
        r0 = 8 + j * CHUNK
        rows = slice(j * CHUNK, (j + 1) * CHUNK)
        for c0 in range(0, CONV_DIM, LANES):
            cols = slice(c0, c0 + LANES)
            xw = cbuf[r0 - 8:r0 + CHUNK, cols]
            xw1 = pltpu.roll(xw, 1, 0)
            near = xw * cw_ref[3:4, cols] + xw1 * cw_ref[2:3, cols]
            far = pltpu.roll(xw * cw_ref[1:2, cols] + xw1 * cw_ref[0:1, cols], 2, 0)
            acc = (near + far)[8:] + cb_ref[:, cols]
            ubuf[:, cols] = acc * _sigmoid(acc)

        dt_in = dt_ref[rows, :] + dtb_ref[...]
        dtv = jnp.maximum(dt_in, 0.0) + jnp.log1p(jnp.exp(-jnp.abs(dt_in)))
        cum3 = jnp.dot(tri, jnp.concatenate(_split3(dtv * a_neg), axis=1), preferred_element_type=F32)
        cum = cum3[:, 2 * LANES:] + cum3[:, LANES:2 * LANES] + cum3[:, :LANES]
        terms = jnp.concatenate([jnp.concatenate(_split3(dtv), axis=1), jnp.concatenate(_split3(cum), axis=1)], axis=0)
        both = jnp.dot(terms, e64_ref[...], preferred_element_type=F32)
        dtx_s[...] = both[:CHUNK]
        acs_s[...] = both[CHUNK:]

        bb, bt, cb, cbd = [], [], [], []
        for g in range(SSM_GROUPS):
            bg = ubuf[:, b_off + g * D_STATE:b_off + (g + 1) * D_STATE]
            bb.append(bg.astype(BF16))
            bt.append(bg.T.astype(BF16))
            cb.append(ubuf[:, c_off + g * D_STATE:c_off + (g + 1) * D_STATE].astype(BF16))
            cbd.append(lax.dot_general(cb[g], jnp.concatenate([bb[g], bb[g]], axis=0), NT_DIMS,
                                       preferred_element_type=F32))

        for hp in range(HEAD_PAIRS):
            sl = slice(hp * LANES, (hp + 1) * LANES)
            g0, g1 = (2 * hp) // 3, (2 * hp + 1) // 3
            acs = acs_s[:, sl]
            acs_key = jnp.sum(jnp.where(diag, acs, 0.0), axis=0, keepdims=True)
            lmat = jnp.exp(jnp.where(causal, acs - acs_key, NEG))
            eacs = jnp.exp(acs)
            acs_last = acs[CHUNK - 1:CHUNK, :]
            xp = ubuf[:, sl]
            xdt = xp * dtx_s[:, sl]
            xlo = jnp.where(lo, xdt, 0.0)
            xhi = jnp.where(lo, 0.0, xdt)

            cbp = cbd[g0] if g0 == g1 else jnp.where(lo, cbd[g0], cbd[g1])
            xbd = jnp.concatenate([xlo, xhi], axis=0).astype(BF16)
            yd = jnp.dot((cbp * lmat).astype(BF16), xbd, preferred_element_type=F32)

            ps = st[hp]
            psb = ps.astype(BF16)
            yo = jnp.dot(cb[g0], psb, preferred_element_type=F32)
            if g0 != g1:
                yo = jnp.where(lo, yo, jnp.dot(cb[g1], psb, preferred_element_type=F32))
            ybuf[:, sl] = yd + yo * eacs + dsk_ref[:, sl] * xp

            wx = jnp.exp(acs_last - acs)
            if g0 == g1:
                new = jnp.dot(bt[g0], (xdt * wx).astype(BF16), preferred_element_type=F32)
            else:
                new = (jnp.dot(bt[g0], (xlo * wx).astype(BF16), preferred_element_type=F32)
                       + jnp.dot(bt[g1], (xhi * wx).astype(BF16), preferred_element_type=F32))
            st[hp] = ps * eacs[CHUNK - 1:CHUNK, :] + new

        lane = lax.broadcasted_iota(jnp.int32, (1, LANES), 1)
        blocks = []
        for v in range(SSM_INNER // LANES):
            ga, gb = (v * LANES) // SSM_GROUP_WIDTH, (v * LANES + LANES - 1) // SSM_GROUP_WIDTH
            blocks.append((slice(v * LANES, (v + 1) * LANES), ga, gb, lane < gb * SSM_GROUP_WIDTH - v * LANES))
        ss = [None] * SSM_GROUPS
        for vs, ga, gb, in_ga in blocks:
            zv = z_ref[rows, vs]
            yv = ybuf[:, vs] * (zv * _sigmoid(zv))
            ybuf[:, vs] = yv
            y2 = yv * yv
            parts = [(ga, y2)] if ga == gb else [(ga, jnp.where(in_ga, y2, 0.0)), (gb, jnp.where(in_ga, 0.0, y2))]
            for g, part in parts:
                t = jnp.sum(part, axis=-1, keepdims=True)
                ss[g] = t if ss[g] is None else ss[g] + t
        rs = [lax.rsqrt(t / SSM_GROUP_WIDTH + EPS) for t in ss]
        for vs, ga, gb, in_ga in blocks:
            sc = rs[ga] if ga == gb else jnp.where(in_ga, rs[ga], rs[gb])
            y_ref[rows, vs] = (ybuf[:, vs] * sc * nw_ref[:, vs]).astype(y_ref.dtype)


SSM_PARAMS = ("conv_w", "conv_b", "dt_bias", "a_log", "d_skip", "ssm_norm")


def _head_expand():
    one = np.arange(LANES)[:, None] == (np.arange(SSM_INNER)[None, :] // SSM_HEAD_DIM)
    return jnp.asarray(np.tile(one, (3, 1)), dtype=BF16)


def _ssd_scratch():
    return [pltpu.VMEM((CHUNK, CONV_DIM), F32), pltpu.VMEM((CHUNK, SSM_INNER), F32),
            pltpu.VMEM((CHUNK, SSM_INNER), F32), pltpu.VMEM((CHUNK, SSM_INNER), F32),
            pltpu.VMEM((HEAD_PAIRS, D_STATE, LANES), F32)]


def _state_specs(prev, layer, b, seq):
    shape = (SSM_HEADS, SSM_HEAD_DIM, D_STATE)
    spec = lambda n: pl.BlockSpec((n, 1) + shape, lambda *g: (0, seq(*g), 0, 0, 0))
    ins = [] if prev is None else [spec(layer)]
    return ins, spec(layer + 1), jax.ShapeDtypeStruct((layer + 1, b) + shape, F32)


def _ssd(z, xbc, dt, hist, h0, p, layer, T, prev):
    b, l, _ = z.shape
    tok = lambda w: pl.BlockSpec((1, T, w), lambda bi, i: (bi, i, 0))
    prev_specs, state_spec, state_shape = _state_specs(prev, layer, b, lambda bi, i: bi)
    return pl.pallas_call(
        functools.partial(_ssd_kernel, T=T, has_prev=len(prev_specs)),
        grid=(b, l // T),
        in_specs=[tok(SSM_INNER), tok(CONV_DIM), tok(LANES),
                  pl.BlockSpec((None, 1, 8, CONV_DIM), lambda bi, i: (layer, bi, 0, 0)),
                  pl.BlockSpec((None, 1, SSM_HEADS, SSM_HEAD_DIM, D_STATE), lambda bi, i: (layer, bi, 0, 0, 0)),
                  *[_layer(p[k], layer) for k in SSM_PARAMS],
                  _resident((3 * LANES, SSM_INNER))] + prev_specs,
        out_specs=[tok(SSM_INNER), state_spec],
        out_shape=[jax.ShapeDtypeStruct((b, l, SSM_INNER), BF16), state_shape],
        scratch_shapes=[pltpu.VMEM((8 + T, CONV_DIM), F32)] + _ssd_scratch(),
        compiler_params=pltpu.CompilerParams(dimension_semantics=("parallel", "arbitrary"),
                                             vmem_limit_bytes=VMEM_LIMIT),
        name="conv_ssd",
    )(z, xbc, dt, hist, h0, *[p[k] for k in SSM_PARAMS], _head_expand(), *([] if prev is None else [prev]))


def _ssm_front_kernel(x_ref, nw_ref, wmain_ref, wside_ref, *refs, T, nt, has_prev):
    prm, prev_ref, refs = refs[:7], refs[7] if has_prev else None, refs[7 + has_prev:]
    (y_ref, cq_ref, hout_ref, tail_ref), scr, (zbuf, dtbuf, cbuf) = refs[:4], refs[4:-3], refs[-3:]
    s = pl.program_id(0)
    slot = lax.rem(s, 2)
    prev = 1 - slot
    i = lax.rem(s + nt - 1, nt)

    @pl.when(s == 0)
    def _first():
        zbuf[1] = jnp.zeros((T, SSM_INNER), F32)
        dtbuf[1] = jnp.zeros((T, LANES), F32)
        cbuf[1] = jnp.zeros((8 + T, CONV_DIM), F32)

    @pl.when((s == 0) | (i == 0))
    def _init():
        cbuf[prev, 0:8, :] = jnp.zeros((8, CONV_DIM), F32)
        for hp in range(HEAD_PAIRS):
            scr[-1][hp] = jnp.zeros((D_STATE, LANES), F32)

    h = _rms(x_ref[...], nw_ref[...]).astype(BF16)

    def project(w_ref, c0, width, store):
        return lambda: store(jnp.dot(h, w_ref[:, c0:c0 + width].astype(BF16), preferred_element_type=F32))

    def z_store(c0):
        def store(v):
            zbuf[slot, :, c0:c0 + 512] = v
        return store

    def xbc_store(c0):
        def store(v):
            cbuf[slot, 8:8 + T, c0:c0 + 512] = v
        return store

    def cq_store(v):
        cq_ref[...] = v.astype(cq_ref.dtype)

    def dt_store(v):
        dtbuf[slot] = v

    projections = ([project(wmain_ref, c0, 512, z_store(c0)) for c0 in range(0, SSM_INNER, 512)]
                   + [project(wmain_ref, SSM_INNER + c0, 512, xbc_store(c0)) for c0 in range(0, CONV_DIM, 512)]
                   + [project(wside_ref, 0, MEM_WIDTH, cq_store), project(wside_ref, MEM_WIDTH, LANES, dt_store)])
    _ssd_tile(zbuf.at[prev], dtbuf.at[prev], cbuf.at[prev], y_ref, prm, scr, T, projections)
    cbuf[slot, 0:8, :] = cbuf[prev, T:T + 8, :]

    @pl.when((s > 0) & (i == nt - 1))
    def _fin():
        _store_state(hout_ref, scr[-1], prev_ref)
        tail_ref[0] = cbuf[prev, T:T + 8, :]


def _ssm_front(x, p, i_layer, layer, w_main, b, T, prev):
    m = x.shape[0]
    nt = m // (b * T)
    n = b * nt
    seq = lambda s: jnp.maximum(s - 1, 0) // nt
    prev_specs, state_spec, state_shape = _state_specs(prev, layer, b, seq)
    return pl.pallas_call(
        functools.partial(_ssm_front_kernel, T=T, nt=nt, has_prev=len(prev_specs)),
        grid=(n + 1,),
        in_specs=[pl.BlockSpec((T, D_MODEL), lambda s: (jnp.minimum(s, n - 1), 0)),
                  _layer(p["norm_mix_pre"], i_layer), _resident(w_main.shape), _layer(p["w_in_side"], layer),
                  *[_layer(p[k], layer) for k in SSM_PARAMS],
                  _resident((3 * LANES, SSM_INNER))] + prev_specs,
        out_specs=[pl.BlockSpec((T, SSM_INNER), lambda s: (jnp.maximum(s - 1, 0), 0)),
                   pl.BlockSpec((T, MEM_WIDTH), lambda s: (jnp.minimum(s, n - 1), 0)),
                   state_spec,
                   pl.BlockSpec((1, 8, CONV_DIM), lambda s: (seq(s), 0, 0))],
        out_shape=[jax.ShapeDtypeStruct((m, SSM_INNER), BF16), jax.ShapeDtypeStruct((m, MEM_WIDTH), BF16),
                   state_shape, jax.ShapeDtypeStruct((b, 8, CONV_DIM), F32)],
        scratch_shapes=_ssd_scratch() + [pltpu.VMEM((2, T, SSM_INNER), F32), pltpu.VMEM((2, T, LANES), F32),
                                         pltpu.VMEM((2, 8 + T, CONV_DIM), F32)],
        compiler_params=pltpu.CompilerParams(dimension_semantics=("arbitrary",), vmem_limit_bytes=VMEM_LIMIT),
        name="ssm_front",
    )(x, p["norm_mix_pre"], w_main, p["w_in_side"], *[p[k] for k in SSM_PARAMS], _head_expand(),
      *([] if prev is None else [prev]))


def _tail_kernel(a_ref, cq_ref, mk_ref, mv_ref, x_ref, wout_ref, npost_ref, nfpre_ref, wup_ref, wdn_ref, nfpost_ref,
                 *refs, nb, T):
    n_cast = len(refs) // 2
    o_ref = refs[n_cast]
    _cast_slabs(refs[:n_cast], refs[n_cast + 1:])
    m = nb * T
    cs = []
    for n in range(nb):
        outs = []
        for h in range(MEM_HEADS):
            hs = slice(h * MEM_HEAD_DIM, (h + 1) * MEM_HEAD_DIM)
            head_rows = pl.ds(h, N_MEM, stride=MEM_HEADS)
            mk = mk_ref[n, head_rows, :].astype(BF16)
            mv = mv_ref[n, head_rows, :].astype(BF16)
            s = lax.dot_general(cq_ref[n, :, hs], mk, NT_DIMS, preferred_element_type=F32)
            s = s * (1.0 / math.sqrt(MEM_HEAD_DIM))
            p = jnp.exp(s - jnp.max(s, axis=-1, keepdims=True))
            den = jnp.sum(p, axis=-1, keepdims=True)
            o = jnp.dot(p.astype(BF16), mv, preferred_element_type=F32) * (1.0 / den)
            outs.append(o.astype(BF16))
        cs.append(jnp.concatenate(outs, axis=1))
    c = cs[0] if nb == 1 else jnp.concatenate(cs, axis=0)
    a = a_ref[...].reshape(m, Q_WIDTH)
    mix = jnp.dot(jnp.concatenate([a, c], axis=1), wout_ref[...], preferred_element_type=F32)
    x1 = x_ref[...].reshape(m, D_MODEL) + _rms(mix, npost_ref[...])
    h2 = _rms(x1, nfpre_ref[...]).astype(BF16)
    acc = None
    for f0 in range(0, D_FF, FF_CHUNK):
        u = jnp.maximum(jnp.dot(h2, wup_ref[:, f0:f0 + FF_CHUNK], preferred_element_type=F32), 0.0)
        d = jnp.dot((u * u).astype(BF16), wdn_ref[f0:f0 + FF_CHUNK, :], preferred_element_type=F32)
        acc = d if acc is None else acc + d
    o_ref[...] = (x1 + _rms(acc, nfpost_ref[...])).reshape(nb, T, D_MODEL)


def _tail(a, cq, mk, mv, x, p, layer, w_out, w_up, w_down, nb, T, cast=()):
    b, l, _ = x.shape
    steps = (b // nb) * (l // T)
    tok = lambda w: pl.BlockSpec((nb, T, w), lambda bi, i: (bi, i, 0))
    mem = pl.BlockSpec((None, nb, N_MEM * MEM_HEADS, MEM_HEAD_DIM), lambda bi, i: (layer, bi, 0, 0))
    norm = lambda k: _layer(p[k], layer)
    cast_in, cast_out, cast_shape = _cast_specs(cast, steps, lambda bi, i: bi * (l // T) + i)
    return pl.pallas_call(
        functools.partial(_tail_kernel, nb=nb, T=T),
        grid=(b // nb, l // T),
        in_specs=[tok(Q_WIDTH), tok(MEM_WIDTH), mem, mem, tok(D_MODEL),
                  _resident(w_out.shape), norm("norm_mix_post"), norm("norm_ffn_pre"), _resident(w_up.shape),
                  _resident(w_down.shape), norm("norm_ffn_post")] + cast_in,
        out_specs=[tok(D_MODEL)] + cast_out,
        out_shape=[jax.ShapeDtypeStruct((b, l, D_MODEL), F32)] + cast_shape,
        compiler_params=pltpu.CompilerParams(dimension_semantics=("arbitrary", "arbitrary"),
                                             vmem_limit_bytes=VMEM_LIMIT),
        name="xattn_outproj_mlp",
    )(a, cq, mk, mv, x, w_out, p["norm_mix_post"], p["norm_ffn_pre"], w_up, w_down, p["norm_ffn_post"],
      *[w for w, _, _ in cast])


ATTN_SEGS = ((0, Q_WIDTH), (Q_WIDTH, 2 * KV_WIDTH), (Q_WIDTH + 2 * KV_WIDTH, MEM_WIDTH))
SSM_MAIN_SEGS = ((0, SSM_INNER), (SSM_INNER, CONV_DIM))
SSM_SIDE_SEGS = ((0, MEM_WIDTH), (MEM_WIDTH, LANES))


def _prep_weights(norm_mix_pre, norm_mix_post, norm_ffn_pre, norm_ffn_post, attn_sinks, w_in_ssm, conv_w,
                  conv_b, dt_bias, a_log, d_skip, ssm_norm):
    dt_lo = SSM_INNER + CONV_DIM
    dt_hi = dt_lo + SSM_HEADS
    pad = jnp.zeros((w_in_ssm.shape[0], D_MODEL, LANES - SSM_HEADS), F32)
    w_side = jnp.concatenate([w_in_ssm[..., dt_hi:], w_in_ssm[..., dt_lo:dt_hi], pad], axis=-1)
    vec = lambda v: v.astype(F32).reshape(v.shape[0], 1, -1)
    lane_pad = lambda v: vec(jnp.pad(v, ((0, 0), (0, LANES - SSM_HEADS))))
    return dict(
        norm_mix_pre=vec(norm_mix_pre), norm_mix_post=vec(norm_mix_post), norm_ffn_pre=vec(norm_ffn_pre),
        norm_ffn_post=vec(norm_ffn_post), w_in_side=w_side,
        attn_bias=jnp.stack([_attn_bias_table(attn_sinks[j]) for j in range(attn_sinks.shape[0])]),
        conv_w=conv_w.astype(F32), conv_b=vec(conv_b), dt_bias=lane_pad(dt_bias), a_log=lane_pad(a_log),
        d_skip=vec(jnp.repeat(d_skip, SSM_HEAD_DIM, axis=-1)), ssm_norm=vec(ssm_norm))


BIG_WEIGHTS = ("w_in", "w_out", "w_up", "w_down")


def _trunk(x, mem_k, mem_v, swa_prev, ssm_h0, conv_hist, p, big, raw, *, tm, attn_tile, ssd_tile, tail_tile):
    b, l, _ = x.shape
    m = b * l
    new_kv, new_conv, states = [], [], None
    for i in range(DEPTH):
        j = i // 2
        xf = x.reshape(m, D_MODEL)
        if i % 2 == 0:
            q, kv, cq = _norm_matmul(xf, p["norm_mix_pre"], i, [(big["w_in"][i], ATTN_SEGS, (BF16, F32, BF16))], j, tm)
            kv = kv.reshape(b, l, 2 * KV_WIDTH)
            missing = [k for k in BIG_WEIGHTS if big[k][i] is None]
            a, *copies = _attention(q.reshape(b, l, Q_WIDTH), kv, swa_prev, p["attn_bias"], j, *attn_tile,
                                    [raw[k][i] for k in missing])
            for k, w in zip(missing, copies):
                big[k][i] = w
            new_kv.append(kv)
        elif ssm_h0 is None:
            a, cq, states, conv_tail = _ssm_front(xf, p, i, j, big["w_in"][i], b, ssd_tile, states)
            a = a.reshape(b, l, SSM_INNER)
            new_conv.append(conv_tail[:, 8 - (CONV_W - 1):])
        else:
            z, xbc, cq, dt = _norm_matmul(xf, p["norm_mix_pre"], i,
                                          [(big["w_in"][i], SSM_MAIN_SEGS, (F32, F32)),
                                           (p["w_in_side"], SSM_SIDE_SEGS, (BF16, F32))], j, tm)
            xbc = xbc.reshape(b, l, CONV_DIM)
            a, states = _ssd(z.reshape(b, l, SSM_INNER), xbc, dt.reshape(b, l, LANES), conv_hist, ssm_h0, p, j,
                             ssd_tile, states)
            new_conv.append(xbc[:, l - (CONV_W - 1):])
        cast = [raw[k][i + 1] for k in BIG_WEIGHTS] if raw is not None and i + 1 < DEPTH else []
        x, *copies = _tail(a, cq.reshape(b, l, MEM_WIDTH), mem_k, mem_v, x, p, i, big["w_out"][i], big["w_up"][i],
                           big["w_down"][i], *tail_tile, cast)
        for k, w in zip(BIG_WEIGHTS, copies):
            big[k][i + 1] = w
    return x, new_kv, states, jnp.stack(new_conv)


def kernel(x_prompt, x_sample, mem_prompt, cache_swa_k, cache_swa_v, state_ssm, state_conv, cache_mem_k, cache_mem_v,
           norm_mix_pre, norm_mix_post, norm_ffn_pre, norm_ffn_post, w_in_attn, attn_sinks, w_in_ssm, conv_w, conv_b,
           dt_bias, a_log, d_skip, ssm_norm, mem_norm, w_mem_kv, w_out, w_up, w_down):
    p = _prep_weights(norm_mix_pre, norm_mix_post, norm_ffn_pre, norm_ffn_post, attn_sinks, w_in_ssm, conv_w, conv_b,
                      dt_bias, a_log, d_skip, ssm_norm)
    bp, lp, _ = x_prompt.shape
    bs, ls, _ = x_sample.shape
    n_attn = w_in_attn.shape[0]
    w_in_ssm_t = jnp.swapaxes(w_in_ssm, 1, 2)
    raw = dict(w_in=[(w_in_ssm_t, i // 2, w_in_ssm.shape[2]) if i % 2 else (w_in_attn, i // 2, None)
                     for i in range(DEPTH)],
               w_out=[(w_out, i, None) for i in range(DEPTH)], w_up=[(w_up, i, None) for i in range(DEPTH)],
               w_down=[(w_down, i, None) for i in range(DEPTH)])
    big = {k: [None] * DEPTH for k in BIG_WEIGHTS}
    big["w_in"][0] = w_in_attn[0].astype(BF16)

    mem_rows = N_MEM * MEM_HEADS
    prompt_mem_k, prompt_mem_v = _mem_kv(mem_prompt.reshape(bp * N_MEM, D_MODEL),
                                         mem_norm.astype(F32).reshape(DEPTH, 1, D_MODEL), w_mem_kv, 1024)

    y_prompt, kv_p, prompt_ssm, prompt_conv = _trunk(
        x_prompt, prompt_mem_k.reshape(DEPTH, bp, mem_rows, MEM_HEAD_DIM),
        prompt_mem_v.reshape(DEPTH, bp, mem_rows, MEM_HEAD_DIM), None, None, None, p, big, raw,
        tm=1024, attn_tile=(1, 256), ssd_tile=256, tail_tile=(1, 512))

    keep = cache_swa_k.shape[2]
    swa_prev = jnp.concatenate([cache_swa_k.reshape(n_attn, bs, keep, KV_WIDTH),
                                cache_swa_v.reshape(n_attn, bs, keep, KV_WIDTH)], axis=-1)
    hist = jnp.pad(state_conv, ((0, 0), (0, 0), (8 - (CONV_W - 1), 0), (0, 0)))
    y_sample, kv_s, sample_ssm, sample_conv = _trunk(
        x_sample, cache_mem_k.reshape(DEPTH, bs, mem_rows, MEM_HEAD_DIM),
        cache_mem_v.reshape(DEPTH, bs, mem_rows, MEM_HEAD_DIM), swa_prev, state_ssm, hist, p, big, None,
        tm=512, attn_tile=(4, ls), ssd_tile=ls, tail_tile=(8, ls))

    def split_kv(kvs, rows):
        kv = jnp.stack([kv[:, -rows:] for kv in kvs])
        shape = kv.shape[:3] + (N_KV_HEADS, HEAD_DIM)
        return kv[..., :KV_WIDTH].reshape(shape), kv[..., KV_WIDTH:].reshape(shape)

    prompt_swa_k, prompt_swa_v = split_kv(kv_p, min(WINDOW, lp))
    sample_swa_k, sample_swa_v = split_kv(kv_s, ls)
    mem_shape = (DEPTH, bp, N_MEM, MEM_HEADS, MEM_HEAD_DIM)
    return (y_prompt, y_sample, prompt_swa_k, prompt_swa_v, prompt_ssm, prompt_conv,
            prompt_mem_k.reshape(mem_shape), prompt_mem_v.reshape(mem_shape),
            sample_swa_k, sample_swa_v, sample_ssm, sample_conv)
```

```python
import functools
import math

import numpy as np
import jax
import jax.numpy as jnp
from jax import lax
from jax.experimental import pallas as pl
from jax.experimental.pallas import tpu as pltpu

F32 = jnp.float32
BF16 = jnp.bfloat16

D_MODEL = 1024
DEPTH = 4
CHUNK = 64
N_HEADS = 24
N_KV_HEADS = 3
HEAD_DIM = 64
WINDOW = 128
Q_WIDTH = N_HEADS * HEAD_DIM
KV_WIDTH = N_KV_HEADS * HEAD_DIM
N_MEM = 256
MEM_HEADS = 4
MEM_HEAD_DIM = 128
MEM_WIDTH = MEM_HEADS * MEM_HEAD_DIM
SSM_INNER = 1536
SSM_HEAD_DIM = 64
SSM_HEADS = SSM_INNER // SSM_HEAD_DIM
SSM_GROUPS = 8
SSM_GROUP_WIDTH = SSM_INNER // SSM_GROUPS
D_STATE = 128
CONV_W = 4
CONV_DIM = SSM_INNER + 2 * SSM_GROUPS * D_STATE
D_FF = 4 * D_MODEL
EPS = 1e-6

LANES = 128
HEAD_PAIRS = SSM_HEADS // 2
NEG = -1e30
TRANSPOSE_SLAB = 2 * LANES
FF_CHUNK = 1024
VMEM_LIMIT = 56 * 1024 * 1024

NT_DIMS = (((1,), (1,)), ((), ()))


def _rms(x, w):
    return x * lax.rsqrt(jnp.mean(x * x, axis=-1, keepdims=True) + EPS) * w


def _sigmoid(x):
    return 1.0 / (1.0 + jnp.exp(-x))


def _split3(x):
    a = x.astype(BF16)
    r = x - a.astype(F32)
    b = r.astype(BF16)
    c = (r - b.astype(F32)).astype(BF16)
    return a, b, c


def _resident(shape):
    return pl.BlockSpec(shape, lambda *_: (0,) * len(shape), pipeline_mode=pl.Buffered(1))


def _layer(stacked, layer):
    shape = stacked.shape[1:]
    return pl.BlockSpec((None,) + shape, lambda *_: (layer,) + (0,) * len(shape), pipeline_mode=pl.Buffered(1))


def _cast_specs(cast, steps, slab):
    ins, outs, shapes = [], [], []
    for w, lw, cols in cast:
        if cols is None:
            rows = w.shape[1] // steps
            ins.append(pl.BlockSpec((None, rows, w.shape[2]), lambda *g, lw=lw: (lw, slab(*g), 0)))
            outs.append(pl.BlockSpec((rows, w.shape[2]), lambda *g: (slab(*g), 0)))
            shapes.append(jax.ShapeDtypeStruct(w.shape[1:], BF16))
        else:
            last = -(-cols // TRANSPOSE_SLAB) - 1
            ins.append(pl.BlockSpec((None, TRANSPOSE_SLAB, w.shape[2]),
                                    lambda *g, lw=lw, last=last: (lw, jnp.minimum(slab(*g), last), 0)))
            outs.append(pl.BlockSpec((w.shape[2], TRANSPOSE_SLAB),
                                     lambda *g, last=last: (0, jnp.minimum(slab(*g), last))))
            shapes.append(jax.ShapeDtypeStruct((w.shape[2], cols), BF16))
    return ins, outs, shapes


def _cast_slabs(srcs, dsts):
    for src, dst in zip(srcs, dsts):
        w = src[...]
        dst[...] = (w if w.shape == dst.shape else w.T).astype(dst.dtype)


def _norm_matmul_kernel(x_ref, nw_ref, *refs, plan):
    w_refs, o_refs = refs[:len(plan)], list(refs[len(plan):])
    h = _rms(x_ref[...], nw_ref[...]).astype(BF16)
    for w_ref, segs in zip(w_refs, plan):
        for start, width in segs:
            o_ref = o_refs.pop(0)
            for c0 in range(0, width, 512):
                cw = min(512, width - c0)
                o_ref[:, c0:c0 + cw] = jnp.dot(
                    h, w_ref[:, start + c0:start + c0 + cw].astype(BF16), preferred_element_type=F32).astype(o_ref.dtype)


def _norm_matmul(x, nw, nw_layer, weights, w_layer, tm):
    m, k = x.shape
    plan = tuple(segs for _, segs, _ in weights)
    widths = [wd for _, segs, _ in weights for _, wd in segs]
    dtypes = [dt for _, _, dts in weights for dt in dts]
    return pl.pallas_call(
        functools.partial(_norm_matmul_kernel, plan=plan),
        grid=(m // tm,),
        in_specs=([pl.BlockSpec((tm, k), lambda i: (i, 0)), _layer(nw, nw_layer)]
                  + [_layer(w, w_layer) if w.ndim == 3 else _resident(w.shape) for w, _, _ in weights]),
        out_specs=[pl.BlockSpec((tm, wd), lambda i: (i, 0)) for wd in widths],
        out_shape=[jax.ShapeDtypeStruct((m, wd), dt) for wd, dt in zip(widths, dtypes)],
        compiler_params=pltpu.CompilerParams(dimension_semantics=("parallel",), vmem_limit_bytes=VMEM_LIMIT),
        name="norm_matmul",
    )(x, nw, *[w for w, _, _ in weights])


def _mem_kv_kernel(x_ref, nw_ref, w_ref, k_ref, v_ref, wb):
    tm = x_ref.shape[0]

    @pl.when(pl.program_id(1) == 0)
    def _cast():
        wb[...] = w_ref[...].astype(BF16)

    h = _rms(x_ref[...], nw_ref[...]).astype(BF16)
    for o_ref, c0 in ((k_ref, 0), (v_ref, MEM_WIDTH)):
        r = jnp.dot(h, wb[:, c0:c0 + MEM_WIDTH], preferred_element_type=F32)
        for hd in range(MEM_HEADS):
            o_ref[pl.ds(hd, tm, stride=MEM_HEADS), :] = r[:, hd * MEM_HEAD_DIM:(hd + 1) * MEM_HEAD_DIM]


def _mem_kv(mem, nw, w, tm):
    m, k = mem.shape
    out = jax.ShapeDtypeStruct((DEPTH, m * MEM_HEADS, MEM_HEAD_DIM), F32)
    out_spec = pl.BlockSpec((None, tm * MEM_HEADS, MEM_HEAD_DIM), lambda l, i: (l, i, 0))
    return pl.pallas_call(
        _mem_kv_kernel,
        grid=(DEPTH, m // tm),
        in_specs=[pl.BlockSpec((tm, k), lambda l, i: (i, 0)),
                  pl.BlockSpec((None, 1, k), lambda l, i: (l, 0, 0)),
                  pl.BlockSpec((None, k, 2 * MEM_WIDTH), lambda l, i: (l, 0, 0))],
        out_specs=[out_spec, out_spec],
        out_shape=[out, out],
        scratch_shapes=[pltpu.VMEM((k, 2 * MEM_WIDTH), BF16)],
        compiler_params=pltpu.CompilerParams(dimension_semantics=("arbitrary", "arbitrary"),
                                             vmem_limit_bytes=VMEM_LIMIT),
        name="mem_kv",
    )(mem, nw, w)


def _attn_bias_table(sinks):
    slopes = jnp.asarray(2.0 ** (-8.0 * np.arange(1, N_HEADS + 1) / N_HEADS), dtype=F32).reshape(N_KV_HEADS, 4, 1, 2, 1)
    span = WINDOW + CHUNK
    i = np.arange(CHUNK)[:, None]
    s = np.arange(2 * LANES)[None, :]
    dist = jnp.asarray(np.abs(WINDOW + i - s), dtype=F32)[None, None, :, None, :]
    slot = jnp.asarray(s)[None, None, :, None, :]
    bias = -slopes * dist
    bias = jnp.where(slot < span, bias, NEG)
    bias = jnp.where(slot == span, sinks.astype(F32).reshape(N_KV_HEADS, 4, 1, 2, 1), bias)
    return bias.reshape(N_KV_HEADS, 4 * CHUNK, 4 * LANES)


def _attn_kernel(q_ref, kvc_ref, kvp_ref, bias_ref, *refs, nb, T, mask_first):
    n_cast = (len(refs) - 2) // 2
    o_ref, kvbuf = refs[n_cast], refs[-1]
    _cast_slabs(refs[:n_cast], refs[n_cast + 1:-1])
    first = pl.program_id(1) == 0
    lo = lax.broadcasted_iota(jnp.int32, (1, LANES), 1) < HEAD_DIM
    slot = lax.broadcasted_iota(jnp.int32, (1, 4 * LANES), 1) & (2 * LANES - 1)
    zpad = jnp.zeros((CHUNK, LANES), BF16)
    ones_halves = jnp.where(
        (lax.broadcasted_iota(jnp.int32, (4 * LANES, LANES), 0) < 2 * LANES)
        == (lax.broadcasted_iota(jnp.int32, (4 * LANES, LANES), 1) < HEAD_DIM), 1.0, 0.0).astype(BF16)
    span = WINDOW + CHUNK
    for n in range(nb):
        kvbuf[0:WINDOW, :] = kvp_ref[n]
        kvbuf[WINDOW:WINDOW + T, :] = kvc_ref[n]
        for j in range(T // CHUNK):
            win = kvbuf[j * CHUNK:j * CHUNK + span, :]
            blocks = [win[:, m * LANES:(m + 1) * LANES] for m in range(3)]
            rolled = [pltpu.roll(b, HEAD_DIM, 1) for b in blocks]

            def halves(idx):
                m, half = divmod(idx, 2)
                in_lo, in_hi = (blocks[m], rolled[m]) if half == 0 else (rolled[m], blocks[m])
                return jnp.where(lo, in_lo, 0.0).astype(BF16), jnp.where(lo, 0.0, in_hi).astype(BF16)

            for kvh in range(N_KV_HEADS):
                klo, khi = halves(kvh)
                vlo, vhi = halves(N_KV_HEADS + kvh)
                kbd = jnp.concatenate([klo, zpad, khi, zpad], axis=0)
                c0 = kvh * 4 * LANES
                qs = jnp.concatenate(
                    [q_ref[n, j * CHUNK:(j + 1) * CHUNK, c0 + p * LANES:c0 + (p + 1) * LANES] for p in range(4)],
                    axis=0)
                s = lax.dot_general(qs * (1.0 / math.sqrt(HEAD_DIM)), kbd, NT_DIMS, preferred_element_type=F32)
                s = s + bias_ref[kvh]
                if mask_first and j < WINDOW // CHUNK:
                    s = jnp.where(slot < jnp.where(first, WINDOW - j * CHUNK, 0), NEG, s)
                ps = []
                for e in range(2):
                    se = s[:, e * 2 * LANES:(e + 1) * 2 * LANES]
                    ps.append(jnp.exp(se - jnp.max(se, axis=-1, keepdims=True)).astype(BF16))
                od = jnp.dot(jnp.concatenate(ps, axis=1),
                             jnp.concatenate([jnp.concatenate([vlo, zpad, vhi, zpad], axis=0), ones_halves], axis=1),
                             preferred_element_type=F32)
                o = od[:, :LANES] / od[:, LANES:]
                for p in range(4):
                    o_ref[n, j * CHUNK:(j + 1) * CHUNK, c0 + p * LANES:c0 + (p + 1) * LANES] = (
                        o[p * CHUNK:(p + 1) * CHUNK].astype(o_ref.dtype))


def _attention(q, kv, kv_prev, bias, layer, nb, T, cast=()):
    b, l, _ = q.shape
    cast_in, cast_out, cast_shape = _cast_specs(cast, (b // nb) * (l // T), lambda bi, i: bi * (l // T) + i)
    mask_first = kv_prev is None
    if mask_first:
        prev_arr = kv
        prev_spec = pl.BlockSpec((nb, WINDOW, 2 * KV_WIDTH),
                                 lambda bi, i: (bi, jnp.maximum(i * (T // WINDOW) - 1, 0), 0))
    else:
        prev_arr = kv_prev
        prev_spec = pl.BlockSpec((None, nb, WINDOW, 2 * KV_WIDTH), lambda bi, i: (layer, bi, 0, 0))
    return pl.pallas_call(
        functools.partial(_attn_kernel, nb=nb, T=T, mask_first=mask_first),
        grid=(b // nb, l // T),
        in_specs=[pl.BlockSpec((nb, T, Q_WIDTH), lambda bi, i: (bi, i, 0)),
                  pl.BlockSpec((nb, T, 2 * KV_WIDTH), lambda bi, i: (bi, i, 0)),
                  prev_spec,
                  _layer(bias, layer)] + cast_in,
        out_specs=[pl.BlockSpec((nb, T, Q_WIDTH), lambda bi, i: (bi, i, 0))] + cast_out,
        out_shape=[jax.ShapeDtypeStruct((b, l, Q_WIDTH), BF16)] + cast_shape,
        scratch_shapes=[pltpu.VMEM((WINDOW + T, 2 * KV_WIDTH), F32)],
        compiler_params=pltpu.CompilerParams(dimension_semantics=("parallel", "parallel"),
                                             vmem_limit_bytes=VMEM_LIMIT),
        name="swa_attention",
    )(q, kv, prev_arr, bias, *[w for w, _, _ in cast])


def _load_state(st, h0_ref):
    for hp in range(HEAD_PAIRS):
        st[hp] = h0_ref[0, 2 * hp:2 * hp + 2].reshape(2 * SSM_HEAD_DIM, D_STATE).T


def _store_state(hout_ref, st, prev_ref):
    n_prev = hout_ref.shape[0] - 1
    if prev_ref is not None:
        hout_ref[0:n_prev] = prev_ref[...]
    for hp in range(HEAD_PAIRS):
        hout_ref[n_prev, 0, 2 * hp:2 * hp + 2] = st[hp].T.reshape(2, SSM_HEAD_DIM, D_STATE)


def _ssd_kernel(z_ref, xbc_ref, dt_ref, hist_ref, h0_ref, *refs, T, has_prev):
    prm, prev_ref, refs = refs[:7], refs[7] if has_prev else None, refs[7 + has_prev:]
    (y_ref, hout_ref, cbuf), scr = refs[:3], refs[3:]
    i = pl.program_id(1)

    @pl.when(i == 0)
    def _init():
        cbuf[0:8, :] = hist_ref[0]
        _load_state(scr[-1], h0_ref)

    cbuf[8:8 + T, :] = xbc_ref[0]
    _ssd_tile(z_ref.at[0], dt_ref.at[0], cbuf, y_ref.at[0], prm, scr, T)
    cbuf[0:8, :] = cbuf[T:T + 8, :]

    @pl.when(i == pl.num_programs(1) - 1)
    def _fin():
        _store_state(hout_ref, scr[-1], prev_ref)


def _ssd_tile(z_ref, dt_ref, cbuf, y_ref, prm, scr, T, side_work=()):
    for work in side_work:
        work()
    cw_ref, cb_ref, dtb_ref, alog_ref, dsk_ref, nw_ref, e64_ref = prm
    ubuf, ybuf, dtx_s, acs_s, st = scr
    row = lax.broadcasted_iota(jnp.int32, (CHUNK, LANES), 0)
    pos = lax.broadcasted_iota(jnp.int32, (CHUNK, LANES), 1) & (CHUNK - 1)
    diag = pos == row
    causal = pos <= row
    lo = lax.broadcasted_iota(jnp.int32, (1, LANES), 1) < SSM_HEAD_DIM
    tri = (lax.broadcasted_iota(jnp.int32, (CHUNK, CHUNK), 1)
           <= lax.broadcasted_iota(jnp.int32, (CHUNK, CHUNK), 0)).astype(F32).astype(BF16)
    a_neg = -jnp.exp(alog_ref[...])
    b_off = SSM_INNER
    c_off = SSM_INNER + SSM_GROUPS * D_STATE

    for j in range(T // CHUNK):
        r0 = 8 + j * CHUNK
        rows = slice(j * CHUNK, (j + 1) * CHUNK)
        for c0 in range(0, CONV_DIM, LANES):
            cols = slice(c0, c0 + LANES)
            xw = cbuf[r0 - 8:r0 + CHUNK, cols]
            xw1 = pltpu.roll(xw, 1, 0)
            near = xw * cw_ref[3:4, cols] + xw1 * cw_ref[2:3, cols]
            far = pltpu.roll(xw * cw_ref[1:2, cols] + xw1 * cw_ref[0:1, cols], 2, 0)
            acc = (near + far)[8:] + cb_ref[:, cols]
            ubuf[:, cols] = acc * _sigmoid(acc)

        dt_in = dt_ref[rows, :] + dtb_ref[...]
        dtv = jnp.maximum(dt_in, 0.0) + jnp.log1p(jnp.exp(-jnp.abs(dt_in)))
        cum3 = jnp.dot(tri, jnp.concatenate(_split3(dtv * a_neg), axis=1), preferred_element_type=F32)
        cum = cum3[:, 2 * LANES:] + cum3[:, LANES:2 * LANES] + cum3[:, :LANES]
        terms = jnp.concatenate([jnp.concatenate(_split3(dtv), axis=1), jnp.concatenate(_split3(cum), axis=1)], axis=0)
        both = jnp.dot(terms, e64_ref[...], preferred_element_type=F32)
        dtx_s[...] = both[:CHUNK]
        acs_s[...] = both[CHUNK:]

        bb, bt, cb, cbd = [], [], [], []
        for g in range(SSM_GROUPS):
            bg = ubuf[:, b_off + g * D_STATE:b_off + (g + 1) * D_STATE]
            bb.append(bg.astype(BF16))
            bt.append(bg.T.astype(BF16))
            cb.append(ubuf[:, c_off + g * D_STATE:c_off + (g + 1) * D_STATE].astype(BF16))
            cbd.append(lax.dot_general(cb[g], jnp.concatenate([bb[g], bb[g]], axis=0), NT_DIMS,
                                       preferred_element_type=F32))

        for hp in range(HEAD_PAIRS):
            sl = slice(hp * LANES, (hp + 1) * LANES)
            g0, g1 = (2 * hp) // 3, (2 * hp + 1) // 3
            acs = acs_s[:, sl]
            acs_key = jnp.sum(jnp.where(diag, acs, 0.0), axis=0, keepdims=True)
            lmat = jnp.exp(jnp.where(causal, acs - acs_key, NEG))
            eacs = jnp.exp(acs)
            acs_last = acs[CHUNK - 1:CHUNK, :]
            xp = ubuf[:, sl]
            xdt = xp * dtx_s[:, sl]
            xlo = jnp.where(lo, xdt, 0.0)
            xhi = jnp.where(lo, 0.0, xdt)

            cbp = cbd[g0] if g0 == g1 else jnp.where(lo, cbd[g0], cbd[g1])
            xbd = jnp.concatenate([xlo, xhi], axis=0).astype(BF16)
            yd = jnp.dot((cbp * lmat).astype(BF16), xbd, preferred_element_type=F32)

            ps = st[hp]
            psb = ps.astype(BF16)
            yo = jnp.dot(cb[g0], psb, preferred_element_type=F32)
            if g0 != g1:
                yo = jnp.where(lo, yo, jnp.dot(cb[g1], psb, preferred_element_type=F32))
            ybuf[:, sl] = yd + yo * eacs + dsk_ref[:, sl] * xp

            wx = jnp.exp(acs_last - acs)
            if g0 == g1:
                new = jnp.dot(bt[g0], (xdt * wx).astype(BF16), preferred_element_type=F32)
            else:
                new = (jnp.dot(bt[g0], (xlo * wx).astype(BF16), preferred_element_type=F32)
                       + jnp.dot(bt[g1], (xhi * wx).astype(BF16), preferred_element_type=F32))
            st[hp] = ps * eacs[CHUNK - 1:CHUNK, :] + new

        lane = lax.broadcasted_iota(jnp.int32, (1, LANES), 1)
        blocks = []
        for v in range(SSM_INNER // LANES):
            ga, gb = (v * LANES) // SSM_GROUP_WIDTH, (v * LANES + LANES - 1) // SSM_GROUP_WIDTH
            blocks.append((slice(v * LANES, (v + 1) * LANES), ga, gb, lane < gb * SSM_GROUP_WIDTH - v * LANES))
        ss = [None] * SSM_GROUPS
        for vs, ga, gb, in_ga in blocks:
            zv = z_ref[rows, vs]
            yv = ybuf[:, vs] * (zv * _sigmoid(zv))
            ybuf[:, vs] = yv
            y2 = yv * yv
            parts = [(ga, y2)] if ga == gb else [(ga, jnp.where(in_ga, y2, 0.0)), (gb, jnp.where(in_ga, 0.0, y2))]
            for g, part in parts:
                t = jnp.sum(part, axis=-1, keepdims=True)
                ss[g] = t if ss[g] is None else ss[g] + t
        rs = [lax.rsqrt(t / SSM_GROUP_WIDTH + EPS) for t in ss]
        for vs, ga, gb, in_ga in blocks:
            sc = rs[ga] if ga == gb else jnp.where(in_ga, rs[ga], rs[gb])
            y_ref[rows, vs] = (ybuf[:, vs] * sc * nw_ref[:, vs]).astype(y_ref.dtype)


SSM_PARAMS = ("conv_w", "conv_b", "dt_bias", "a_log", "d_skip", "ssm_norm")


def _head_expand():
    one = np.arange(LANES)[:, None] == (np.arange(SSM_INNER)[None, :] // SSM_HEAD_DIM)
    return jnp.asarray(np.tile(one, (3, 1)), dtype=BF16)


def _ssd_scratch():
    return [pltpu.VMEM((CHUNK, CONV_DIM), F32), pltpu.VMEM((CHUNK, SSM_INNER), F32),
            pltpu.VMEM((CHUNK, SSM_INNER), F32), pltpu.VMEM((CHUNK, SSM_INNER), F32),
            pltpu.VMEM((HEAD_PAIRS, D_STATE, LANES), F32)]


def _state_specs(prev, layer, b, seq):
    shape = (SSM_HEADS, SSM_HEAD_DIM, D_STATE)
    spec = lambda n: pl.BlockSpec((n, 1) + shape, lambda *g: (0, seq(*g), 0, 0, 0))
    ins = [] if prev is None else [spec(layer)]
    return ins, spec(layer + 1), jax.ShapeDtypeStruct((layer + 1, b) + shape, F32)


def _ssd(z, xbc, dt, hist, h0, p, layer, T, prev):
    b, l, _ = z.shape
    tok = lambda w: pl.BlockSpec((1, T, w), lambda bi, i: (bi, i, 0))
    prev_specs, state_spec, state_shape = _state_specs(prev, layer, b, lambda bi, i: bi)
    return pl.pallas_call(
        functools.partial(_ssd_kernel, T=T, has_prev=len(prev_specs)),
        grid=(b, l // T),
        in_specs=[tok(SSM_INNER), tok(CONV_DIM), tok(LANES),
                  pl.BlockSpec((None, 1, 8, CONV_DIM), lambda bi, i: (layer, bi, 0, 0)),
                  pl.BlockSpec((None, 1, SSM_HEADS, SSM_HEAD_DIM, D_STATE), lambda bi, i: (layer, bi, 0, 0, 0)),
                  *[_layer(p[k], layer) for k in SSM_PARAMS],
                  _resident((3 * LANES, SSM_INNER))] + prev_specs,
        out_specs=[tok(SSM_INNER), state_spec],
        out_shape=[jax.ShapeDtypeStruct((b, l, SSM_INNER), BF16), state_shape],
        scratch_shapes=[pltpu.VMEM((8 + T, CONV_DIM), F32)] + _ssd_scratch(),
        compiler_params=pltpu.CompilerParams(dimension_semantics=("parallel", "arbitrary"),
                                             vmem_limit_bytes=VMEM_LIMIT),
        name="conv_ssd",
    )(z, xbc, dt, hist, h0, *[p[k] for k in SSM_PARAMS], _head_expand(), *([] if prev is None else [prev]))


def _ssm_front_kernel(x_ref, nw_ref, wmain_ref, wside_ref, *refs, T, nt, has_prev):
    prm, prev_ref, refs = refs[:7], refs[7] if has_prev else None, refs[7 + has_prev:]
    (y_ref, cq_ref, hout_ref, tail_ref), scr, (zbuf, dtbuf, cbuf) = refs[:4], refs[4:-3], refs[-3:]
    s = pl.program_id(0)
    slot = lax.rem(s, 2)
    prev = 1 - slot
    i = lax.rem(s + nt - 1, nt)

    @pl.when(s == 0)
    def _first():
        zbuf[1] = jnp.zeros((T, SSM_INNER), F32)
        dtbuf[1] = jnp.zeros((T, LANES), F32)
        cbuf[1] = jnp.zeros((8 + T, CONV_DIM), F32)

    @pl.when((s == 0) | (i == 0))
    def _init():
        cbuf[prev, 0:8, :] = jnp.zeros((8, CONV_DIM), F32)
        for hp in range(HEAD_PAIRS):
            scr[-1][hp] = jnp.zeros((D_STATE, LANES), F32)

    h = _rms(x_ref[...], nw_ref[...]).astype(BF16)

    def project(w_ref, c0, width, store):
        return lambda: store(jnp.dot(h, w_ref[:, c0:c0 + width].astype(BF16), preferred_element_type=F32))

    def z_store(c0):
        def store(v):
            zbuf[slot, :, c0:c0 + 512] = v
        return store

    def xbc_store(c0):
        def store(v):
            cbuf[slot, 8:8 + T, c0:c0 + 512] = v
        return store

    def cq_store(v):
        cq_ref[...] = v.astype(cq_ref.dtype)

    def dt_store(v):
        dtbuf[slot] = v

    projections = ([project(wmain_ref, c0, 512, z_store(c0)) for c0 in range(0, SSM_INNER, 512)]
                   + [project(wmain_ref, SSM_INNER + c0, 512, xbc_store(c0)) for c0 in range(0, CONV_DIM, 512)]
                   + [project(wside_ref, 0, MEM_WIDTH, cq_store), project(wside_ref, MEM_WIDTH, LANES, dt_store)])
    _ssd_tile(zbuf.at[prev], dtbuf.at[prev], cbuf.at[prev], y_ref, prm, scr, T, projections)
    cbuf[slot, 0:8, :] = cbuf[prev, T:T + 8, :]

    @pl.when((s > 0) & (i == nt - 1))
    def _fin():
        _store_state(hout_ref, scr[-1], prev_ref)
        tail_ref[0] = cbuf[prev, T:T + 8, :]


def _ssm_front(x, p, i_layer, layer, w_main, b, T, prev):
    m = x.shape[0]
    nt = m // (b * T)
    n = b * nt
    seq = lambda s: jnp.maximum(s - 1, 0) // nt
    prev_specs, state_spec, state_shape = _state_specs(prev, layer, b, seq)
    return pl.pallas_call(
        functools.partial(_ssm_front_kernel, T=T, nt=nt, has_prev=len(prev_specs)),
        grid=(n + 1,),
        in_specs=[pl.BlockSpec((T, D_MODEL), lambda s: (jnp.minimum(s, n - 1), 0)),
                  _layer(p["norm_mix_pre"], i_layer), _resident(w_main.shape), _layer(p["w_in_side"], layer),
                  *[_layer(p[k], layer) for k in SSM_PARAMS],
                  _resident((3 * LANES, SSM_INNER))] + prev_specs,
        out_specs=[pl.BlockSpec((T, SSM_INNER), lambda s: (jnp.maximum(s - 1, 0), 0)),
                   pl.BlockSpec((T, MEM_WIDTH), lambda s: (jnp.minimum(s, n - 1), 0)),
                   state_spec,
                   pl.BlockSpec((1, 8, CONV_DIM), lambda s: (seq(s), 0, 0))],
        out_shape=[jax.ShapeDtypeStruct((m, SSM_INNER), BF16), jax.ShapeDtypeStruct((m, MEM_WIDTH), BF16),
                   state_shape, jax.ShapeDtypeStruct((b, 8, CONV_DIM), F32)],
        scratch_shapes=_ssd_scratch() + [pltpu.VMEM((2, T, SSM_INNER), F32), pltpu.VMEM((2, T, LANES), F32),
                                         pltpu.VMEM((2, 8 + T, CONV_DIM), F32)],
        compiler_params=pltpu.CompilerParams(dimension_semantics=("arbitrary",), vmem_limit_bytes=VMEM_LIMIT),
        name="ssm_front",
    )(x, p["norm_mix_pre"], w_main, p["w_in_side"], *[p[k] for k in SSM_PARAMS], _head_expand(),
      *([] if prev is None else [prev]))


def _tail_kernel(a_ref, cq_ref, mk_ref, mv_ref, x_ref, wout_ref, npost_ref, nfpre_ref, wup_ref, wdn_ref, nfpost_ref,
                 *refs, nb, T):
    n_cast = len(refs) // 2
    o_ref = refs[n_cast]
    _cast_slabs(refs[:n_cast], refs[n_cast + 1:])
    m = nb * T
    cs = []
    for n in range(nb):
        outs = []
        for h in range(MEM_HEADS):
            hs = slice(h * MEM_HEAD_DIM, (h + 1) * MEM_HEAD_DIM)
            head_rows = pl.ds(h, N_MEM, stride=MEM_HEADS)
            mk = mk_ref[n, head_rows, :].astype(BF16)
            mv = jnp.concatenate([mv_ref[n, head_rows, :].astype(BF16), jnp.ones((N_MEM, MEM_HEAD_DIM), BF16)], axis=1)
            s = lax.dot_general(cq_ref[n, :, hs], mk, NT_DIMS, preferred_element_type=F32)
            s = s * (1.0 / math.sqrt(MEM_HEAD_DIM))
            p = jnp.exp(s - jnp.max(s, axis=-1, keepdims=True))
            od = jnp.dot(p.astype(BF16), mv, preferred_element_type=F32)
            outs.append((od[:, :MEM_HEAD_DIM] / od[:, MEM_HEAD_DIM:]).astype(BF16))
        cs.append(jnp.concatenate(outs, axis=1))
    c = cs[0] if nb == 1 else jnp.concatenate(cs, axis=0)
    a = a_ref[...].reshape(m, Q_WIDTH)
    mix = jnp.dot(jnp.concatenate([a, c], axis=1), wout_ref[...], preferred_element_type=F32)
    x1 = x_ref[...].reshape(m, D_MODEL) + _rms(mix, npost_ref[...])
    h2 = _rms(x1, nfpre_ref[...]).astype(BF16)
    acc = None
    for f0 in range(0, D_FF, FF_CHUNK):
        u = jnp.maximum(jnp.dot(h2, wup_ref[:, f0:f0 + FF_CHUNK], preferred_element_type=F32), 0.0)
        d = jnp.dot((u * u).astype(BF16), wdn_ref[f0:f0 + FF_CHUNK, :], preferred_element_type=F32)
        acc = d if acc is None else acc + d
    o_ref[...] = (x1 + _rms(acc, nfpost_ref[...])).reshape(nb, T, D_MODEL)


def _tail(a, cq, mk, mv, x, p, layer, w_out, w_up, w_down, nb, T, cast=()):
    b, l, _ = x.shape
    steps = (b // nb) * (l // T)
    tok = lambda w: pl.BlockSpec((nb, T, w), lambda bi, i: (bi, i, 0))
    mem = pl.BlockSpec((None, nb, N_MEM * MEM_HEADS, MEM_HEAD_DIM), lambda bi, i: (layer, bi, 0, 0))
    norm = lambda k: _layer(p[k], layer)
    cast_in, cast_out, cast_shape = _cast_specs(cast, steps, lambda bi, i: bi * (l // T) + i)
    return pl.pallas_call(
        functools.partial(_tail_kernel, nb=nb, T=T),
        grid=(b // nb, l // T),
        in_specs=[tok(Q_WIDTH), tok(MEM_WIDTH), mem, mem, tok(D_MODEL),
                  _resident(w_out.shape), norm("norm_mix_post"), norm("norm_ffn_pre"), _resident(w_up.shape),
                  _resident(w_down.shape), norm("norm_ffn_post")] + cast_in,
        out_specs=[tok(D_MODEL)] + cast_out,
        out_shape=[jax.ShapeDtypeStruct((b, l, D_MODEL), F32)] + cast_shape,
        compiler_params=pltpu.CompilerParams(dimension_semantics=("arbitrary", "arbitrary"),
                                             vmem_limit_bytes=VMEM_LIMIT),
        name="xattn_outproj_mlp",
    )(a, cq, mk, mv, x, w_out, p["norm_mix_post"], p["norm_ffn_pre"], w_up, w_down, p["norm_ffn_post"],
      *[w for w, _, _ in cast])


ATTN_SEGS = ((0, Q_WIDTH), (Q_WIDTH, 2 * KV_WIDTH), (Q_WIDTH + 2 * KV_WIDTH, MEM_WIDTH))
SSM_MAIN_SEGS = ((0, SSM_INNER), (SSM_INNER, CONV_DIM))
SSM_SIDE_SEGS = ((0, MEM_WIDTH), (MEM_WIDTH, LANES))


def _prep_weights(norm_mix_pre, norm_mix_post, norm_ffn_pre, norm_ffn_post, attn_sinks, w_in_ssm, conv_w,
                  conv_b, dt_bias, a_log, d_skip, ssm_norm):
    dt_lo = SSM_INNER + CONV_DIM
    dt_hi = dt_lo + SSM_HEADS
    pad = jnp.zeros((w_in_ssm.shape[0], D_MODEL, LANES - SSM_HEADS), F32)
    w_side = jnp.concatenate([w_in_ssm[..., dt_hi:], w_in_ssm[..., dt_lo:dt_hi], pad], axis=-1)
    vec = lambda v: v.astype(F32).reshape(v.shape[0], 1, -1)
    lane_pad = lambda v: vec(jnp.pad(v, ((0, 0), (0, LANES - SSM_HEADS))))
    return dict(
        norm_mix_pre=vec(norm_mix_pre), norm_mix_post=vec(norm_mix_post), norm_ffn_pre=vec(norm_ffn_pre),
        norm_ffn_post=vec(norm_ffn_post), w_in_side=w_side,
        attn_bias=jnp.stack([_attn_bias_table(attn_sinks[j]) for j in range(attn_sinks.shape[0])]),
        conv_w=conv_w.astype(F32), conv_b=vec(conv_b), dt_bias=lane_pad(dt_bias), a_log=lane_pad(a_log),
        d_skip=vec(jnp.repeat(d_skip, SSM_HEAD_DIM, axis=-1)), ssm_norm=vec(ssm_norm))


BIG_WEIGHTS = ("w_in", "w_out", "w_up", "w_down")


def _trunk(x, mem_k, mem_v, swa_prev, ssm_h0, conv_hist, p, big, raw, *, tm, attn_tile, ssd_tile, tail_tile):
    b, l, _ = x.shape
    m = b * l
    new_kv, new_conv, states = [], [], None
    for i in range(DEPTH):
        j = i // 2
        xf = x.reshape(m, D_MODEL)
        if i % 2 == 0:
            q, kv, cq = _norm_matmul(xf, p["norm_mix_pre"], i, [(big["w_in"][i], ATTN_SEGS, (BF16, F32, BF16))], j, tm)
            kv = kv.reshape(b, l, 2 * KV_WIDTH)
            missing = [k for k in BIG_WEIGHTS if big[k][i] is None]
            a, *copies = _attention(q.reshape(b, l, Q_WIDTH), kv, swa_prev, p["attn_bias"], j, *attn_tile,
                                    [raw[k][i] for k in missing])
            for k, w in zip(missing, copies):
                big[k][i] = w
            new_kv.append(kv)
        elif ssm_h0 is None:
            a, cq, states, conv_tail = _ssm_front(xf, p, i, j, big["w_in"][i], b, ssd_tile, states)
            a = a.reshape(b, l, SSM_INNER)
            new_conv.append(conv_tail[:, 8 - (CONV_W - 1):])
        else:
            z, xbc, cq, dt = _norm_matmul(xf, p["norm_mix_pre"], i,
                                          [(big["w_in"][i], SSM_MAIN_SEGS, (F32, F32)),
                                           (p["w_in_side"], SSM_SIDE_SEGS, (BF16, F32))], j, tm)
            xbc = xbc.reshape(b, l, CONV_DIM)
            a, states = _ssd(z.reshape(b, l, SSM_INNER), xbc, dt.reshape(b, l, LANES), conv_hist, ssm_h0, p, j,
                             ssd_tile, states)
            new_conv.append(xbc[:, l - (CONV_W - 1):])
        cast = [raw[k][i + 1] for k in BIG_WEIGHTS] if raw is not None and i + 1 < DEPTH else []
        x, *copies = _tail(a, cq.reshape(b, l, MEM_WIDTH), mem_k, mem_v, x, p, i, big["w_out"][i], big["w_up"][i],
                           big["w_down"][i], *tail_tile, cast)
        for k, w in zip(BIG_WEIGHTS, copies):
            big[k][i + 1] = w
    return x, new_kv, states, jnp.stack(new_conv)


def kernel(x_prompt, x_sample, mem_prompt, cache_swa_k, cache_swa_v, state_ssm, state_conv, cache_mem_k, cache_mem_v,
           norm_mix_pre, norm_mix_post, norm_ffn_pre, norm_ffn_post, w_in_attn, attn_sinks, w_in_ssm, conv_w, conv_b,
           dt_bias, a_log, d_skip, ssm_norm, mem_norm, w_mem_kv, w_out, w_up, w_down):
    p = _prep_weights(norm_mix_pre, norm_mix_post, norm_ffn_pre, norm_ffn_post, attn_sinks, w_in_ssm, conv_w, conv_b,
                      dt_bias, a_log, d_skip, ssm_norm)
    bp, lp, _ = x_prompt.shape
    bs, ls, _ = x_sample.shape
    n_attn = w_in_attn.shape[0]
    w_in_ssm_t = jnp.swapaxes(w_in_ssm, 1, 2)
    raw = dict(w_in=[(w_in_ssm_t, i // 2, w_in_ssm.shape[2]) if i % 2 else (w_in_attn, i // 2, None)
                     for i in range(DEPTH)],
               w_out=[(w_out, i, None) for i in range(DEPTH)], w_up=[(w_up, i, None) for i in range(DEPTH)],
               w_down=[(w_down, i, None) for i in range(DEPTH)])
    big = {k: [None] * DEPTH for k in BIG_WEIGHTS}
    big["w_in"][0] = w_in_attn[0].astype(BF16)

    mem_rows = N_MEM * MEM_HEADS
    prompt_mem_k, prompt_mem_v = _mem_kv(mem_prompt.reshape(bp * N_MEM, D_MODEL),
                                         mem_norm.astype(F32).reshape(DEPTH, 1, D_MODEL), w_mem_kv, 1024)

    y_prompt, kv_p, prompt_ssm, prompt_conv = _trunk(
        x_prompt, prompt_mem_k.reshape(DEPTH, bp, mem_rows, MEM_HEAD_DIM),
        prompt_mem_v.reshape(DEPTH, bp, mem_rows, MEM_HEAD_DIM), None, None, None, p, big, raw,
        tm=1024, attn_tile=(1, 256), ssd_tile=256, tail_tile=(1, 512))

    keep = cache_swa_k.shape[2]
    swa_prev = jnp.concatenate([cache_swa_k.reshape(n_attn, bs, keep, KV_WIDTH),
                                cache_swa_v.reshape(n_attn, bs, keep, KV_WIDTH)], axis=-1)
    hist = jnp.pad(state_conv, ((0, 0), (0, 0), (8 - (CONV_W - 1), 0), (0, 0)))
    y_sample, kv_s, sample_ssm, sample_conv = _trunk(
        x_sample, cache_mem_k.reshape(DEPTH, bs, mem_rows, MEM_HEAD_DIM),
        cache_mem_v.reshape(DEPTH, bs, mem_rows, MEM_HEAD_DIM), swa_prev, state_ssm, hist, p, big, None,
        tm=512, attn_tile=(4, ls), ssd_tile=ls, tail_tile=(8, ls))

    def split_kv(kvs, rows):
        kv = jnp.stack([kv[:, -rows:] for kv in kvs])
        shape = kv.shape[:3] + (N_KV_HEADS, HEAD_DIM)
        return kv[..., :KV_WIDTH].reshape(shape), kv[..., KV_WIDTH:].reshape(shape)

    prompt_swa_k, prompt_swa_v = split_kv(kv_p, min(WINDOW, lp))
    sample_swa_k, sample_swa_v = split_kv(kv_s, ls)
    mem_shape = (DEPTH, bp, N_MEM, MEM_HEADS, MEM_HEAD_DIM)
    return (y_prompt, y_sample, prompt_swa_k, prompt_swa_v, prompt_ssm, prompt_conv,
            prompt_mem_k.reshape(mem_shape), prompt_mem_v.reshape(mem_shape),
            sample_swa_k, sample_swa_v, sample_ssm, sample_conv)
```
